```python
import math
import jax, jax.numpy as jnp
from jax import lax
import numpy as np

D_MODEL = 1024
BATCH = 16
SEQ = 4096
DEPTH = 2

N_MIXERS = 2
N_SUB = 3
D_FF = 2816
MLSTM_HEADS = 8
MLSTM_DQK = D_MODEL // (2 * MLSTM_HEADS)
MLSTM_DV = D_MODEL // MLSTM_HEADS
MLSTM_CHUNK = 64
CONV_WIDTH = 4
MLSTM_QK_COLS = 2 * MLSTM_HEADS * MLSTM_DQK
MLSTM_V_COLS = MLSTM_HEADS * MLSTM_DV
MLSTM_PROJ = MLSTM_QK_COLS + 2 * MLSTM_V_COLS + 2 * MLSTM_HEADS
DIL_GROUPS = ((128, 1), (512, 4), (2048, 16))
N_GROUPS = len(DIL_GROUPS)
ATTN_HEADS = 8
ATTN_HEAD_DIM = D_MODEL // ATTN_HEADS
ATTN_PROJ = N_GROUPS * 3 * ATTN_HEADS * ATTN_HEAD_DIM
ALPHA = (2 * DEPTH) ** 0.25
BETA = (8 * DEPTH) ** -0.25
LN_EPS = 1e-5
RMS_EPS = 1e-6
N_MLSTM_LAYERS = (DEPTH + 1) // 2
N_ATTN_LAYERS = DEPTH // 2

kernel_name = 'hybrid_mlstm_dilated_attn_macaron_deepnorm_adaln'


def _layer_norm(x, g, b):
    xf = x.astype(jnp.float32)
    mu = xf.mean(-1, keepdims=True)
    var = jnp.square(xf - mu).mean(-1, keepdims=True)
    return ((xf - mu) * lax.rsqrt(var + LN_EPS) * g.astype(jnp.float32) + b.astype(jnp.float32)).astype(x.dtype)


def _swiglu(h, w_in, w_out):
    g, u = jnp.split(h @ w_in, 2, axis=-1)
    return (jax.nn.silu(g) * u) @ w_out


def _causal_dwconv(x, w):
    k_w, ch = w.shape
    xp = jnp.pad(x, ((0, 0), (k_w - 1, 0), (0, 0)))
    return lax.conv_general_dilated(xp, w[:, None, :].astype(x.dtype), window_strides=(1,), padding='VALID',
                                    dimension_numbers=('NWC', 'WIO', 'NWC'), feature_group_count=ch)


def _mlstm_cell_chunkwise(q, k, v, i_pre, log_f):
    B, S, H, DK = q.shape
    DV = v.shape[-1]
    L = MLSTM_CHUNK
    nc = S // L

    def chunks(t):
        return jnp.moveaxis(t.reshape((B, nc, L, H) + t.shape[3:]), 3, 1)

    qc = chunks(q.astype(jnp.float32)) * (DK ** -0.5)
    kc = chunks(k.astype(jnp.float32))
    vc = chunks(v.astype(jnp.float32))
    ig = chunks(i_pre)
    bcum = jnp.cumsum(chunks(log_f), axis=-1)
    b_last = bcum[..., -1]

    a = b_last[..., None] - bcum + ig
    m_loc = a.max(-1)
    wa = jnp.exp(a - m_loc[..., None])
    c_loc = jnp.einsum('bhcl,bhclv,bhclk->bhcvk', wa, vc, kc)
    n_loc = jnp.einsum('bhcl,bhclk->bhck', wa, kc)

    def step(carry, inp):
        c_st, n_st, m_st = carry
        cl, nl, ml, bl = inp
        m_new = jnp.maximum(bl + m_st, ml)
        sp = jnp.exp(bl + m_st - m_new)
        sl = jnp.exp(ml - m_new)
        c_new = sp[..., None, None] * c_st + sl[..., None, None] * cl
        n_new = sp[..., None] * n_st + sl[..., None] * nl
        return (c_new, n_new, m_new), (c_st, n_st, m_st)

    init = (jnp.zeros((B, H, DV, DK), jnp.float32), jnp.zeros((B, H, DK), jnp.float32),
            jnp.zeros((B, H), jnp.float32))
    xs = (jnp.moveaxis(c_loc, 2, 0), jnp.moveaxis(n_loc, 2, 0), jnp.moveaxis(m_loc, 2, 0), jnp.moveaxis(b_last, 2, 0))
    _, (c_prev, n_prev, m_prev) = lax.scan(step, init, xs)
    c_prev = jnp.moveaxis(c_prev, 0, 2)
    n_prev = jnp.moveaxis(n_prev, 0, 2)
    m_prev = jnp.moveaxis(m_prev, 0, 2)

    causal = jnp.tril(jnp.ones((L, L), dtype=bool))
    dlog = jnp.where(causal, bcum[..., :, None] - bcum[..., None, :] + ig[..., None, :], -jnp.inf)
    inter_log = bcum + m_prev[..., None]
    m = jnp.maximum(inter_log, dlog.max(-1))
    sc = jnp.einsum('bhcik,bhcjk->bhcij', qc, kc) * jnp.exp(dlog - m[..., None])
    inter_w = jnp.exp(inter_log - m)
    num = jnp.einsum('bhcij,bhcjv->bhciv', sc, vc) + inter_w[..., None] * jnp.einsum('bhcik,bhcvk->bhciv', qc, c_prev)
    den = sc.sum(-1) + inter_w * jnp.einsum('bhcik,bhck->bhci', qc, n_prev)
    h = num / jnp.maximum(jnp.abs(den), jnp.exp(-m))[..., None]
    return jnp.moveaxis(h, 1, 3).reshape(B, S, H, DV)


def _mlstm_mixer(h, w_in, gate_bias, conv_w, head_gain, w_out):
    B, S, _ = h.shape
    proj = h @ w_in
    qk = proj[..., :MLSTM_QK_COLS]
    v = proj[..., MLSTM_QK_COLS:MLSTM_QK_COLS + MLSTM_V_COLS]
    o = proj[..., MLSTM_QK_COLS + MLSTM_V_COLS:MLSTM_QK_COLS + 2 * MLSTM_V_COLS]
    gates = proj[..., MLSTM_QK_COLS + 2 * MLSTM_V_COLS:].astype(jnp.float32) + gate_bias.astype(jnp.float32)
    qk = jax.nn.silu(_causal_dwconv(qk, conv_w))
    q = qk[..., :MLSTM_QK_COLS // 2].reshape(B, S, MLSTM_HEADS, MLSTM_DQK)
    k = qk[..., MLSTM_QK_COLS // 2:].reshape(B, S, MLSTM_HEADS, MLSTM_DQK)
    v = v.reshape(B, S, MLSTM_HEADS, MLSTM_DV)
    i_pre = gates[..., :MLSTM_HEADS]
    log_f = jax.nn.log_sigmoid(gates[..., MLSTM_HEADS:])
    ht = _mlstm_cell_chunkwise(q, k, v, i_pre, log_f)
    ht = ht * lax.rsqrt(jnp.mean(jnp.square(ht), axis=-1, keepdims=True) + RMS_EPS)
    ht = ht.reshape(B, S, MLSTM_V_COLS) * head_gain.astype(jnp.float32)
    y = jax.nn.sigmoid(o.astype(jnp.float32)) * ht
    return y.astype(h.dtype) @ w_out


def _dilated_window_attention(q, k, v, window, dilation):
    B, S, H, Dh = q.shape
    blk = window // dilation
    unit = blk * dilation
    s_pad = -(-S // unit) * unit
    nb = s_pad // unit

    def to_blocks(t):
        t = jnp.pad(t.astype(jnp.float32), ((0, 0), (0, s_pad - S), (0, 0), (0, 0)))
        t = jnp.swapaxes(t.reshape(B, s_pad // dilation, dilation, H, Dh), 1, 2)
        return t.reshape(B, dilation, nb, blk, H, Dh)

    def with_prev(t):
        prev = jnp.pad(t, ((0, 0), (0, 0), (1, 0), (0, 0), (0, 0), (0, 0)))[:, :, :-1]
        return jnp.concatenate([prev, t], axis=3)

    qb = to_blocks(q)
    kc = with_prev(to_blocks(k))
    vc = with_prev(to_blocks(v))
    s = jnp.einsum('brnqhd,brnkhd->brnhqk', qb, kc) * (Dh ** -0.5)
    qi = jnp.arange(blk)[:, None]
    ki = jnp.arange(2 * blk)[None, :]
    band = (ki >= qi) & (ki <= qi + blk)
    has_prev = (jnp.arange(nb) > 0)[:, None, None] | (ki >= blk)[None]
    mask = band[None] & has_prev
    s = jnp.where(mask[:, None], s, -jnp.inf)
    mx = s.max(-1, keepdims=True)
    e = jnp.exp(s - mx)
    den = e.sum(-1)
    o = jnp.einsum('brnhqk,brnkhd->brnqhd', e, vc) / jnp.swapaxes(den, 3, 4)[..., None]
    lse = jnp.swapaxes(mx[..., 0] + jnp.log(den), 3, 4)

    def from_blocks(t):
        t = t.reshape((B, dilation, s_pad // dilation) + t.shape[4:])
        t = jnp.swapaxes(t, 1, 2).reshape((B, s_pad) + t.shape[3:])
        return t[:, :S]

    return from_blocks(o), from_blocks(lse)


def _dilated_mixer(h, w_in, w_out):
    B, S, _ = h.shape
    proj = (h @ w_in).reshape(B, S, N_GROUPS, 3, ATTN_HEADS, ATTN_HEAD_DIM)
    outs, lses = [], []
    for g, (window, dilation) in enumerate(DIL_GROUPS):
        o_g, lse_g = _dilated_window_attention(proj[:, :, g, 0], proj[:, :, g, 1], proj[:, :, g, 2], window, dilation)
        outs.append(o_g)
        lses.append(lse_g)
    wts = jax.nn.softmax(jnp.stack(lses, 0), axis=0)
    o = jnp.sum(wts[..., None] * jnp.stack(outs, 0), axis=0)
    return o.reshape(B, S, ATTN_HEADS * ATTN_HEAD_DIM).astype(h.dtype) @ w_out


def _dense(key, fan_in, shape, scale=1.0):
    return jax.random.normal(key, shape, jnp.float32) * (scale * fan_in ** -0.5)


def setup_inputs(seed: int = 0) -> dict:
    key = jax.random.key(seed)
    ks = jax.random.split(key, 20)
    na, nb_ = N_MLSTM_LAYERS, N_ATTN_LAYERS
    x = jax.random.normal(ks[0], (BATCH, SEQ, D_MODEL), jnp.float32)
    c = jax.random.normal(ks[1], (BATCH, D_MODEL), jnp.float32)
    ada_w = _dense(ks[2], D_MODEL, (DEPTH, D_MODEL, N_SUB * 3 * D_MODEL), 0.1)
    ada_b = 0.01 * jax.random.normal(ks[3], (DEPTH, N_SUB * 3 * D_MODEL), jnp.float32)
    ln_g = 1.0 + 0.02 * jax.random.normal(ks[4], (DEPTH, N_SUB, D_MODEL), jnp.float32)
    ln_b = 0.02 * jax.random.normal(ks[5], (DEPTH, N_SUB, D_MODEL), jnp.float32)
    ffn_w_in = _dense(ks[6], D_MODEL, (DEPTH, 2, D_MODEL, 2 * D_FF))
    ffn_w_out = _dense(ks[7], D_FF, (DEPTH, 2, D_FF, D_MODEL), BETA)
    mlstm_w_in = jnp.concatenate([
        _dense(ks[8], D_MODEL, (na, D_MODEL, MLSTM_QK_COLS)),
        _dense(ks[9], D_MODEL, (na, D_MODEL, MLSTM_V_COLS), BETA),
        _dense(ks[10], D_MODEL, (na, D_MODEL, MLSTM_V_COLS + 2 * MLSTM_HEADS)),
    ], axis=-1)
    mlstm_gate_bias = jnp.concatenate([
        0.1 * jax.random.normal(ks[11], (na, MLSTM_HEADS), jnp.float32),
        3.0 + 3.0 * jax.random.uniform(ks[12], (na, MLSTM_HEADS), jnp.float32),
    ], axis=-1)
    mlstm_conv_w = _dense(ks[13], CONV_WIDTH, (na, CONV_WIDTH, MLSTM_QK_COLS))
    mlstm_head_gain = 1.0 + 0.02 * jax.random.normal(ks[14], (na, MLSTM_V_COLS), jnp.float32)
    mlstm_w_out = _dense(ks[15], MLSTM_V_COLS, (na, MLSTM_V_COLS, D_MODEL), BETA)
    qkv_scale = jnp.array([1.0, 1.0, BETA], jnp.float32)[:, None]
    attn_w_in = (_dense(ks[16], D_MODEL, (nb_, D_MODEL, N_GROUPS, 3, ATTN_HEADS * ATTN_HEAD_DIM)) * qkv_scale
                 ).reshape(nb_, D_MODEL, ATTN_PROJ)
    attn_w_out = _dense(ks[17], ATTN_HEADS * ATTN_HEAD_DIM, (nb_, ATTN_HEADS * ATTN_HEAD_DIM, D_MODEL), BETA)
    return {'x': x, 'c': c, 'ada_w': ada_w, 'ada_b': ada_b, 'ln_g': ln_g, 'ln_b': ln_b,
            'ffn_w_in': ffn_w_in, 'ffn_w_out': ffn_w_out, 'mlstm_w_in': mlstm_w_in,
            'mlstm_gate_bias': mlstm_gate_bias, 'mlstm_conv_w': mlstm_conv_w,
            'mlstm_head_gain': mlstm_head_gain, 'mlstm_w_out': mlstm_w_out,
            'attn_w_in': attn_w_in, 'attn_w_out': attn_w_out}


def reference(x, c, ada_w, ada_b, ln_g, ln_b, ffn_w_in, ffn_w_out, mlstm_w_in, mlstm_gate_bias,
              mlstm_conv_w, mlstm_head_gain, mlstm_w_out, attn_w_in, attn_w_out):
    B = x.shape[0]
    cond = jax.nn.silu(c)
    for layer in range(DEPTH):
        mod = (cond @ ada_w[layer] + ada_b[layer]).reshape(B, N_SUB, 3, 1, D_MODEL)

        def modulate(h, s):
            return h * (1.0 + mod[:, s, 1]) + mod[:, s, 0]

        def post(h, out, s, weight):
            return _layer_norm(ALPHA * h + weight * (1.0 + mod[:, s, 2]) * out, ln_g[layer, s], ln_b[layer, s])

        x = post(x, _swiglu(modulate(x, 0), ffn_w_in[layer, 0], ffn_w_out[layer, 0]), 0, 0.5)
        j = layer // N_MIXERS
        if layer % N_MIXERS == 0:
            y = _mlstm_mixer(modulate(x, 1), mlstm_w_in[j], mlstm_gate_bias[j], mlstm_conv_w[j],
                             mlstm_head_gain[j], mlstm_w_out[j])
        else:
            y = _dilated_mixer(modulate(x, 1), attn_w_in[j], attn_w_out[j])
        x = post(x, y, 1, 1.0)
        x = post(x, _swiglu(modulate(x, 2), ffn_w_in[layer, 1], ffn_w_out[layer, 1]), 2, 0.5)
    return x
```

```python
import functools

import jax
import jax.numpy as jnp
from jax import lax
from jax.experimental import pallas as pl
from jax.experimental.pallas import tpu as pltpu

F32 = jnp.float32
BF16 = jnp.bfloat16

DEPTH = 2
N_SUB = 3
D_FF = 2816
MLSTM_HEADS = 8
MLSTM_DQK = 64
MLSTM_DV = 128
CONV_WIDTH = 4
DIL_GROUPS = ((128, 1), (512, 4), (2048, 16))
ATTN_HEADS = 8
ATTN_HEAD_DIM = 128
ALPHA = (2 * DEPTH) ** 0.25
LN_EPS = 1e-5
RMS_EPS = 1e-6

LANES = 128
MAX_DIL = 16
CELL_CHUNK = 128
ATT_BLK = 128
MIB = 1024 * 1024


def _cparams(sem, vmem_mib):
    return pltpu.CompilerParams(dimension_semantics=sem, vmem_limit_bytes=int(vmem_mib * MIB))


def _resident(shape):
    nd = len(shape)
    return pl.BlockSpec(shape, lambda *_: (0,) * nd, pipeline_mode=pl.Buffered(1))


def _layer_norm(y, g, b):
    mu = jnp.mean(y, axis=-1, keepdims=True)
    yc = y - mu
    var = jnp.mean(yc * yc, axis=-1, keepdims=True)
    return yc * lax.rsqrt(var + LN_EPS) * g + b


def _modulate(x, mod_ref, s):
    return x * (1.0 + mod_ref[3 * s + 1:3 * s + 2, :]) + mod_ref[3 * s:3 * s + 1, :]


def _post(x, out, mod_ref, s, weight, lng_ref, lnb_ref):
    y = ALPHA * x + (weight * (1.0 + mod_ref[3 * s + 2:3 * s + 3, :])) * out
    return _layer_norm(y, lng_ref[...], lnb_ref[...])


def _ada_body(c_ref, w_ref, b_ref, o_ref):
    c = c_ref[...]
    cond = c * jax.nn.sigmoid(c)
    o_ref[...] = jnp.dot(cond, w_ref[...], precision=lax.Precision.HIGHEST,
                         preferred_element_type=F32) + b_ref[...]


def _ada(c, ada_w, ada_b):
    depth, d, n = ada_w.shape
    b = c.shape[0]
    tn = 1152
    return pl.pallas_call(
        _ada_body,
        grid=(depth, n // tn),
        in_specs=[pl.BlockSpec((b, d), lambda l, j: (0, 0)),
                  pl.BlockSpec((None, d, tn), lambda l, j: (l, 0, j)),
                  pl.BlockSpec((None, 1, tn), lambda l, j: (l, 0, j))],
        out_specs=pl.BlockSpec((None, b, tn), lambda l, j: (l, 0, j)),
        out_shape=jax.ShapeDtypeStruct((depth, b, n), F32),
        compiler_params=_cparams(("arbitrary", "arbitrary"), 32),
        name="ada_mod",
    )(c, ada_w, ada_b.reshape(depth, 1, n))


def _ffn_body(x_ref, mod_ref, lng_ref, lnb_ref, wg_ref, wu_ref, wo_ref, o_ref, *, s):
    x = x_ref[...]
    h = _modulate(x, mod_ref, s).astype(BF16)
    g = jnp.dot(h, wg_ref[...], preferred_element_type=F32)
    u = jnp.dot(h, wu_ref[...], preferred_element_type=F32)
    a = (g * jax.nn.sigmoid(g) * u).astype(BF16)
    out = jnp.dot(a, wo_ref[...], preferred_element_type=F32)
    o_ref[...] = _post(x, out, mod_ref, s, 0.5, lng_ref, lnb_ref)


def _ffn(x, mod, lng, lnb, w_in, w_out, s, tm=512):
    b, sq, d = x.shape
    wg = w_in[:, :D_FF].astype(BF16)
    wu = w_in[:, D_FF:].astype(BF16)
    wo = w_out.astype(BF16)
    row = pl.BlockSpec((None, tm, d), lambda i, j: (i, j, 0))
    return pl.pallas_call(
        functools.partial(_ffn_body, s=s),
        grid=(b, sq // tm),
        in_specs=[row,
                  pl.BlockSpec((None, 3 * N_SUB, d), lambda i, j: (i, 0, 0)),
                  _resident((1, d)), _resident((1, d)),
                  _resident(wg.shape), _resident(wu.shape), _resident(wo.shape)],
        out_specs=row,
        out_shape=jax.ShapeDtypeStruct(x.shape, F32),
        compiler_params=_cparams(("parallel", "parallel"), 56),
        name=f"ffn_s{s}",
    )(x, mod, lng.reshape(1, d), lnb.reshape(1, d), wg, wu, wo)


def _mproj_body(x_ref, mod_ref, w_ref, wgate_ref, gbias_ref, qk_ref, v_ref, o_ref, gates_ref):
    d = x_ref.shape[-1]
    h = _modulate(x_ref[...], mod_ref, 1).astype(BF16)
    p = jnp.dot(h, w_ref[...], preferred_element_type=F32)
    qk_ref[...] = p[:, :d].astype(BF16)
    v_ref[...] = p[:, d:2 * d].astype(BF16)
    o_ref[...] = p[:, 2 * d:].astype(BF16)
    gates_ref[...] = jnp.dot(h, wgate_ref[...], preferred_element_type=F32) + gbias_ref[...]


def _mlstm_proj(x, mod, w_in, gate_bias, tm=512):
    b, sq, d = x.shape
    n_main = 3 * d
    n_gate = 2 * MLSTM_HEADS
    w = w_in[:, :n_main].astype(BF16)
    wgate = jnp.pad(w_in[:, n_main:], ((0, 0), (0, LANES - n_gate))).astype(BF16)
    gbias = jnp.pad(gate_bias.astype(F32), (0, LANES - n_gate)).reshape(1, LANES)
    row = pl.BlockSpec((None, tm, d), lambda i, j: (i, j, 0))
    act = jax.ShapeDtypeStruct((b, sq, d), BF16)
    return pl.pallas_call(
        _mproj_body,
        grid=(b, sq // tm),
        in_specs=[row,
                  pl.BlockSpec((None, 3 * N_SUB, d), lambda i, j: (i, 0, 0)),
                  _resident(w.shape), _resident(wgate.shape), _resident(gbias.shape)],
        out_specs=[row, row, row, pl.BlockSpec((None, tm, LANES), lambda i, j: (i, j, 0))],
        out_shape=[act, act, act, jax.ShapeDtypeStruct((b, sq, LANES), F32)],
        compiler_params=_cparams(("parallel", "parallel"), 48),
        name="mlstm_proj",
    )(x, mod, w, wgate, gbias)


def _cell_body(qk_ref, v_ref, o_ref, gates_ref, convw_ref, gain_ref, y_ref, xbuf, cst, mst):
    L = qk_ref.shape[0]
    nh, dk, dv = MLSTM_HEADS, MLSTM_DQK, MLSTM_DV
    c = pl.program_id(1)

    @pl.when(c == 0)
    def _():
        xbuf[0:8, :] = jnp.zeros((8, xbuf.shape[1]), F32)
        cst[...] = jnp.zeros(cst.shape, F32)
        mst[...] = jnp.zeros(mst.shape, F32)

    @pl.when(c > 0)
    def _():
        xbuf[0:8, :] = xbuf[L:L + 8, :]

    xbuf[8:L + 8, :] = qk_ref[...].astype(F32)
    conv = convw_ref[0:1, :] * xbuf[5:5 + L, :]
    for i in range(1, CONV_WIDTH):
        conv = conv + convw_ref[i:i + 1, :] * xbuf[5 + i:5 + i + L, :]
    qk = conv * jax.nn.sigmoid(conv)

    gt = gates_ref[...]
    logf = jnp.minimum(gt, 0.0) - jnp.log1p(jnp.exp(-jnp.abs(gt)))
    ri = lax.broadcasted_iota(jnp.int32, (L, L), 0)
    ci = lax.broadcasted_iota(jnp.int32, (L, L), 1)
    causal = ri >= ci
    tril = jnp.where(causal, 1.0, 0.0).astype(F32)
    bcum = jnp.dot(tril, logf, precision=lax.Precision.HIGHEST, preferred_element_type=F32)
    lane = lax.broadcasted_iota(jnp.int32, (L, LANES), 1)
    z = jnp.where(lane < nh, gt, bcum)
    zt = z.T

    lane_l = lax.broadcasted_iota(jnp.int32, (L, LANES), 1)
    ones_col = jnp.where(lane_l == 0, 1.0, 0.0).astype(BF16)

    for p in range(nh // 2):
        qp = qk[:, p * LANES:(p + 1) * LANES]
        kp = qk[:, nh * dk + p * LANES:nh * dk + (p + 1) * LANES]
        kt = kp.T
        kt_b = kt.astype(BF16)
        c_pair = jnp.concatenate([cst[2 * p], cst[2 * p + 1]], axis=0).astype(BF16)
        for half in range(2):
            h = 2 * p + half
            q_h = (jnp.where((lane_l // dk) == half, qp, 0.0) * (dk ** -0.5)).astype(BF16)
            v_aug = jnp.concatenate([v_ref[:, h * dv:(h + 1) * dv], ones_col], axis=1)
            bcol = bcum[:, nh + h:nh + h + 1]
            brow = zt[nh + h:nh + h + 1, :]
            igrow = zt[h:h + 1, :]
            m_prev = mst[h:h + 1, 0:1]

            dlog = jnp.where(causal, bcol - brow + igrow, -jnp.inf)
            inter_log = bcol + m_prev
            m = jnp.maximum(inter_log, jnp.max(dlog, axis=1, keepdims=True))
            s = jnp.dot(q_h, kt_b, preferred_element_type=F32)
            pm = (s * jnp.exp(dlog - m)).astype(BF16)
            inter_w = jnp.exp(inter_log - m)
            num_aug = (jnp.dot(pm, v_aug, preferred_element_type=F32)
                       + inter_w * jnp.dot(q_h, c_pair, preferred_element_type=F32))
            num = num_aug[:, :dv]
            den = num_aug[:, dv:dv + 1]
            hh = num / jnp.maximum(jnp.abs(den), jnp.exp(-m))
            hh = hh * lax.rsqrt(jnp.mean(hh * hh, axis=-1, keepdims=True) + RMS_EPS)
            hh = hh * gain_ref[:, h * dv:(h + 1) * dv]
            og = o_ref[:, h * dv:(h + 1) * dv].astype(F32)
            y_ref[:, h * dv:(h + 1) * dv] = (jax.nn.sigmoid(og) * hh).astype(BF16)

            b_last = brow[:, L - 1:L]
            a = b_last - brow + igrow
            m_loc = jnp.max(a, axis=1, keepdims=True)
            wa = jnp.exp(a - m_loc)
            kw = (kt[half * dk:(half + 1) * dk, :] * wa).astype(BF16)
            c_loc = jnp.dot(kw, v_aug, preferred_element_type=F32)
            m_new = jnp.maximum(b_last + m_prev, m_loc)
            sp = jnp.exp(b_last + m_prev - m_new)
            sl = jnp.exp(m_loc - m_new)
            cst[h] = sp * cst[h] + sl * c_loc
            mst[h:h + 1, :] = jnp.broadcast_to(m_new, (1, LANES))


def _mlstm_cell(qk, v, o, gates, conv_w, head_gain):
    b, sq, d = v.shape
    L = CELL_CHUNK
    row = pl.BlockSpec((None, L, d), lambda i, j: (i, j, 0))
    return pl.pallas_call(
        _cell_body,
        grid=(b, sq // L),
        in_specs=[row, row, row,
                  pl.BlockSpec((None, L, LANES), lambda i, j: (i, j, 0)),
                  _resident((CONV_WIDTH, d)), _resident((1, d))],
        out_specs=row,
        out_shape=jax.ShapeDtypeStruct((b, sq, d), BF16),
        scratch_shapes=[pltpu.VMEM((L + 8, d), F32),
                        pltpu.VMEM((MLSTM_HEADS, MLSTM_DQK, 2 * MLSTM_DV), F32),
                        pltpu.VMEM((MLSTM_HEADS, LANES), F32)],
        compiler_params=_cparams(("parallel", "arbitrary"), 32),
        name="mlstm_cell",
    )(qk, v, o, gates, conv_w.astype(F32), head_gain.astype(F32).reshape(1, d))


def _mixer_out_body(y_ref, x_ref, mod_ref, lng_ref, lnb_ref, w_ref, o_ref):
    out = jnp.dot(y_ref[...], w_ref[...], preferred_element_type=F32)
    o_ref[...] = _post(x_ref[...], out, mod_ref, 1, 1.0, lng_ref, lnb_ref)


def _mixer_out(y, x, mod, lng, lnb, w_out, tm=512):
    b, sq, d = x.shape
    w = w_out.astype(BF16)
    row = pl.BlockSpec((None, tm, d), lambda i, j: (i, j, 0))
    return pl.pallas_call(
        _mixer_out_body,
        grid=(b, sq // tm),
        in_specs=[row, row,
                  pl.BlockSpec((None, 3 * N_SUB, d), lambda i, j: (i, 0, 0)),
                  _resident((1, d)), _resident((1, d)), _resident(w.shape)],
        out_specs=row,
        out_shape=jax.ShapeDtypeStruct(x.shape, F32),
        compiler_params=_cparams(("parallel", "parallel"), 32),
        name="mlstm_out",
    )(y, x, mod, lng.reshape(1, d), lnb.reshape(1, d), w)


def _aproj_body(x_ref, mod_ref, w_ref, o_ref):
    h = _modulate(x_ref[...], mod_ref, 1).astype(BF16)
    o_ref[...] = jnp.dot(h, w_ref[...], preferred_element_type=F32).astype(BF16)


def _attn_proj(x, mod, w_bf16, g, dil, tm=256):
    b, sq, d = x.shape
    n = 3 * d
    rows = sq // dil
    xv = x.reshape(b, rows, dil * d)
    return pl.pallas_call(
        _aproj_body,
        grid=(b, dil, rows // tm),
        in_specs=[pl.BlockSpec((None, tm, d), lambda i, r, j: (i, j, r)),
                  pl.BlockSpec((None, 3 * N_SUB, d), lambda i, r, j: (i, 0, 0)),
                  pl.BlockSpec((d, n), lambda i, r, j: (0, g), pipeline_mode=pl.Buffered(1))],
        out_specs=pl.BlockSpec((None, None, tm, n), lambda i, r, j: (i, r, j, 0)),
        out_shape=jax.ShapeDtypeStruct((b, dil, rows, n), BF16),
        compiler_params=_cparams(("parallel", "parallel", "parallel"), 40),
        name=f"attn_proj_g{g}",
    )(xv, mod, w_bf16)


def _attn_body(q_ref, kc_ref, kp_ref, vc_ref, vp_ref, o_ref, lse_ref):
    blk = q_ref.shape[0]
    dh = ATTN_HEAD_DIM
    n = pl.program_id(2)
    qi = lax.broadcasted_iota(jnp.int32, (blk, blk), 0)
    kj = lax.broadcasted_iota(jnp.int32, (blk, blk), 1)
    mask_prev = (kj >= qi) & (n > 0)
    mask_cur = kj <= qi
    lane = lax.broadcasted_iota(jnp.int32, (blk, LANES), 1)
    lse_tile = jnp.zeros((blk, LANES), F32)
    nt = (((1,), (1,)), ((), ()))
    for h in range(ATTN_HEADS):
        cs = slice(h * dh, (h + 1) * dh)
        q = q_ref[:, cs]
        sp = lax.dot_general(q, kp_ref[:, cs], nt, preferred_element_type=F32) * (dh ** -0.5)
        sc = lax.dot_general(q, kc_ref[:, cs], nt, preferred_element_type=F32) * (dh ** -0.5)
        sp = jnp.where(mask_prev, sp, -jnp.inf)
        sc = jnp.where(mask_cur, sc, -jnp.inf)
        mx = jnp.maximum(jnp.max(sp, axis=1, keepdims=True), jnp.max(sc, axis=1, keepdims=True))
        ep = jnp.exp(sp - mx)
        ec = jnp.exp(sc - mx)
        den = jnp.sum(ep, axis=1, keepdims=True) + jnp.sum(ec, axis=1, keepdims=True)
        acc = (jnp.dot(ep.astype(BF16), vp_ref[:, cs], preferred_element_type=F32)
               + jnp.dot(ec.astype(BF16), vc_ref[:, cs], preferred_element_type=F32))
        o_ref[:, cs] = (acc / den).astype(BF16)
        lse_tile = jnp.where(lane == h, mx + jnp.log(den), lse_tile)
    lse_ref[...] = lse_tile


def _attn(proj, d):
    b, dil, rows, _ = proj.shape
    blk = ATT_BLK

    def spec(col, prev):
        if prev:
            return pl.BlockSpec((None, None, blk, d), lambda i, r, j: (i, r, jnp.maximum(j - 1, 0), col))
        return pl.BlockSpec((None, None, blk, d), lambda i, r, j: (i, r, j, col))

    return pl.pallas_call(
        _attn_body,
        grid=(b, dil, rows // blk),
        in_specs=[spec(0, False), spec(1, False), spec(1, True), spec(2, False), spec(2, True)],
        out_specs=[pl.BlockSpec((None, None, blk, d), lambda i, r, j: (i, r, j, 0)),
                   pl.BlockSpec((None, None, blk, LANES), lambda i, r, j: (i, r, j, 0))],
        out_shape=[jax.ShapeDtypeStruct((b, dil, rows, d), BF16),
                   jax.ShapeDtypeStruct((b, dil, rows, LANES), F32)],
        compiler_params=_cparams(("parallel", "parallel", "arbitrary"), 32),
        name=f"attn_d{dil}",
    )(proj, proj, proj, proj, proj)


def _amerge_body(o1_ref, o4_ref, o16_ref, l1_ref, l4_ref, l16_ref, x_ref, mod_ref, lng_ref, lnb_ref,
                 w_ref, out_ref, ybuf):
    dh = ATTN_HEAD_DIM
    o_refs = (o1_ref, o4_ref, o16_ref)
    lses = [r[...] for r in (l1_ref, l4_ref, l16_ref)]
    for h in range(ATTN_HEADS):
        cs = slice(h * dh, (h + 1) * dh)
        ls = [t[:, h:h + 1] for t in lses]
        mx = jnp.maximum(jnp.maximum(ls[0], ls[1]), ls[2])
        ws = [jnp.exp(t - mx) for t in ls]
        tot = ws[0] + ws[1] + ws[2]
        acc = (ws[0] / tot) * o_refs[0][:, cs].astype(F32)
        for g in (1, 2):
            acc = acc + (ws[g] / tot) * o_refs[g][:, cs].astype(F32)
        ybuf[:, cs] = acc.astype(BF16)
    out = jnp.dot(ybuf[...], w_ref[...], preferred_element_type=F32)
    out_ref[...] = _post(x_ref[...], out, mod_ref, 1, 1.0, lng_ref, lnb_ref)


def _attn_merge_out(o_list, lse_list, x, mod, lng, lnb, w_out, tm=256):
    b, sq, d = x.shape
    o1, o4, o16 = o_list
    l1, l4, l16 = lse_list
    rows = sq // MAX_DIL
    w = w_out.astype(BF16)
    o1v = o1.reshape(b, rows, MAX_DIL * d)
    l1v = l1.reshape(b, rows, MAX_DIL * LANES)
    o4v = o4.reshape(b, 4, rows, 4 * d)
    l4v = l4.reshape(b, 4, rows, 4 * LANES)
    xv = x.reshape(b, rows, MAX_DIL * d)

    def nat(width):
        return pl.BlockSpec((None, tm, width), lambda i, r, j: (i, j, r))

    def g4(width):
        return pl.BlockSpec((None, None, tm, width), lambda i, r, j: (i, r % 4, j, r // 4))

    def g16(width):
        return pl.BlockSpec((None, None, tm, width), lambda i, r, j: (i, r, j, 0))

    out = pl.pallas_call(
        _amerge_body,
        grid=(b, MAX_DIL, rows // tm),
        in_specs=[nat(d), g4(d), g16(d), nat(LANES), g4(LANES), g16(LANES), nat(d),
                  pl.BlockSpec((None, 3 * N_SUB, d), lambda i, r, j: (i, 0, 0)),
                  _resident((1, d)), _resident((1, d)), _resident(w.shape)],
        out_specs=nat(d),
        out_shape=jax.ShapeDtypeStruct(xv.shape, F32),
        scratch_shapes=[pltpu.VMEM((tm, d), BF16)],
        compiler_params=_cparams(("parallel", "parallel", "parallel"), 32),
        name="attn_merge_out",
    )(o1v, o4v, o16, l1v, l4v, l16, xv, mod, lng.reshape(1, d), lnb.reshape(1, d), w)
    return out.reshape(b, sq, d)


def kernel(x, c, ada_w, ada_b, ln_g, ln_b, ffn_w_in, ffn_w_out, mlstm_w_in, mlstm_gate_bias, mlstm_conv_w,
           mlstm_head_gain, mlstm_w_out, attn_w_in, attn_w_out):
    b, sq, d = x.shape
    assert len(DIL_GROUPS) == 3 and all(w // dl == ATT_BLK for w, dl in DIL_GROUPS)
    assert sq % (MAX_DIL * 256) == 0 and d == MLSTM_HEADS * MLSTM_DV == ATTN_HEADS * ATTN_HEAD_DIM
    mod_all = _ada(c, ada_w, ada_b)
    for layer in range(DEPTH):
        mod = mod_all[layer].reshape(b, 3 * N_SUB, d)
        x = _ffn(x, mod, ln_g[layer, 0], ln_b[layer, 0], ffn_w_in[layer, 0], ffn_w_out[layer, 0], 0)
        j = layer // 2
        if layer % 2 == 0:
            qk, v, o, gates = _mlstm_proj(x, mod, mlstm_w_in[j], mlstm_gate_bias[j])
            y = _mlstm_cell(qk, v, o, gates, mlstm_conv_w[j], mlstm_head_gain[j])
            x = _mixer_out(y, x, mod, ln_g[layer, 1], ln_b[layer, 1], mlstm_w_out[j])
        else:
            w_bf16 = attn_w_in[j].astype(BF16)
            outs, lses = [], []
            for g, (_, dil) in enumerate(DIL_GROUPS):
                proj = _attn_proj(x, mod, w_bf16, g, dil)
                o_g, lse_g = _attn(proj, d)
                outs.append(o_g)
                lses.append(lse_g)
            x = _attn_merge_out(outs, lses, x, mod, ln_g[layer, 1], ln_b[layer, 1], attn_w_out[j])
        x = _ffn(x, mod, ln_g[layer, 2], ln_b[layer, 2], ffn_w_in[layer, 1], ffn_w_out[layer, 1], 2)
    return x
```

```python
import functools

import numpy as np
import jax
import jax.numpy as jnp
from jax import lax
from jax.experimental import pallas as pl
from jax.experimental.pallas import tpu as pltpu

F32 = jnp.float32
BF16 = jnp.bfloat16

DEPTH = 2
N_SUB = 3
D_FF = 2816
MLSTM_HEADS = 8
MLSTM_DQK = 64
MLSTM_DV = 128
CONV_WIDTH = 4
DIL_GROUPS = ((128, 1), (512, 4), (2048, 16))
ATTN_HEADS = 8
ATTN_HEAD_DIM = 128
ALPHA = (2 * DEPTH) ** 0.25
LN_EPS = 1e-5
RMS_EPS = 1e-6

LANES = 128
NCLS = 16
CELL_CHUNK = 128
ATT_SPAN = 128
ATT_QB = 256
MIB = 1024 * 1024


def _cparams(sem, vmem_mib):
    return pltpu.CompilerParams(dimension_semantics=sem, vmem_limit_bytes=int(vmem_mib * MIB))


def _resident(shape):
    nd = len(shape)
    return pl.BlockSpec(shape, lambda *_: (0,) * nd, pipeline_mode=pl.Buffered(1))


def _layer_norm(y, g, b):
    mu = jnp.mean(y, axis=-1, keepdims=True)
    yc = y - mu
    var = jnp.mean(yc * yc, axis=-1, keepdims=True)
    return yc * lax.rsqrt(var + LN_EPS) * g + b


def _modulate(x, mod_ref, s):
    return x * (1.0 + mod_ref[3 * s + 1:3 * s + 2, :]) + mod_ref[3 * s:3 * s + 1, :]


def _post(x, out, mod_ref, s, weight, lng_ref, lnb_ref):
    y = ALPHA * x + (weight * (1.0 + mod_ref[3 * s + 2:3 * s + 3, :])) * out
    return _layer_norm(y, lng_ref[...], lnb_ref[...])


def _ada_body(c_ref, w_ref, b_ref, o_ref):
    c = c_ref[...]
    cond = c * jax.nn.sigmoid(c)
    o_ref[...] = jnp.dot(cond, w_ref[...], precision=lax.Precision.HIGHEST,
                         preferred_element_type=F32) + b_ref[...]


def _ada(c, ada_w, ada_b):
    depth, d, n = ada_w.shape
    b = c.shape[0]
    tn = 1152
    return pl.pallas_call(
        _ada_body,
        grid=(depth, n // tn),
        in_specs=[pl.BlockSpec((b, d), lambda l, j: (0, 0)),
                  pl.BlockSpec((None, d, tn), lambda l, j: (l, 0, j)),
                  pl.BlockSpec((None, 1, tn), lambda l, j: (l, 0, j))],
        out_specs=pl.BlockSpec((None, b, tn), lambda l, j: (l, 0, j)),
        out_shape=jax.ShapeDtypeStruct((depth, b, n), F32),
        compiler_params=_cparams(("arbitrary", "arbitrary"), 32),
        name="ada_mod",
    )(c, ada_w, ada_b.reshape(depth, 1, n))


def _ffn_body(x_ref, mod_ref, lng_ref, lnb_ref, wg_ref, wu_ref, wo_ref, o_ref, *, s, order_in, order_out):
    if order_in == "cls":
        tm = x_ref.shape[0] * x_ref.shape[1]
        x = x_ref[...].reshape(tm, x_ref.shape[2])
    elif order_out == "cls":
        tm, d = x_ref.shape
        x = jnp.swapaxes(x_ref[...].reshape(tm // NCLS, NCLS, d), 0, 1).reshape(tm, d)
    else:
        tm = x_ref.shape[0]
        x = x_ref[...]
    h = _modulate(x, mod_ref, s).astype(BF16)
    g = jnp.dot(h, wg_ref[...], preferred_element_type=F32)
    u = jnp.dot(h, wu_ref[...], preferred_element_type=F32)
    a = (g * jax.nn.sigmoid(g) * u).astype(BF16)
    out = jnp.dot(a, wo_ref[...], preferred_element_type=F32)
    y = _post(x, out, mod_ref, s, 0.5, lng_ref, lnb_ref)
    if order_out == "cls":
        o_ref[...] = y.reshape(o_ref.shape)
    elif order_in == "cls":
        o_ref[...] = jnp.swapaxes(y.reshape(NCLS, tm // NCLS, y.shape[1]), 0, 1).reshape(o_ref.shape)
    else:
        o_ref[...] = y


def _ffn(x, mod, lng, lnb, w_in, w_out, s, order_in="nat", order_out="nat", tm=512):
    d = x.shape[-1]
    b = x.shape[0]
    sq = x.shape[1] if order_in == "nat" else x.shape[1] * x.shape[2]
    wg = w_in[:, :D_FF].astype(BF16)
    wu = w_in[:, D_FF:].astype(BF16)
    wo = w_out.astype(BF16)
    nat = pl.BlockSpec((None, tm, d), lambda i, j: (i, j, 0))
    cls = pl.BlockSpec((None, NCLS, tm // NCLS, d), lambda i, j: (i, 0, j, 0))
    out_shape = (b, sq, d) if order_out == "nat" else (b, NCLS, sq // NCLS, d)
    return pl.pallas_call(
        functools.partial(_ffn_body, s=s, order_in=order_in, order_out=order_out),
        grid=(b, sq // tm),
        in_specs=[cls if order_in == "cls" else nat,
                  pl.BlockSpec((None, 3 * N_SUB, d), lambda i, j: (i, 0, 0)),
                  _resident((1, d)), _resident((1, d)),
                  _resident(wg.shape), _resident(wu.shape), _resident(wo.shape)],
        out_specs=cls if order_out == "cls" else nat,
        out_shape=jax.ShapeDtypeStruct(out_shape, F32),
        compiler_params=_cparams(("parallel", "parallel"), 56),
        name=f"ffn_s{s}_{order_in}_{order_out}",
    )(x, mod, lng.reshape(1, d), lnb.reshape(1, d), wg, wu, wo)


def _mproj_body(x_ref, mod_ref, w_ref, wgate_ref, gbias_ref, convw_ref,
                q_ref, kt_ref, v_ref, o_ref, bc_ref, cm_ref, gt_ref, pbuf):
    tm, d = x_ref.shape
    nh, dk = MLSTM_HEADS, MLSTM_DQK
    L = CELL_CHUNK
    j = pl.program_id(1)

    @pl.when(j == 0)
    def _():
        pbuf[0:8, :] = jnp.zeros((8, d), F32)

    @pl.when(j > 0)
    def _():
        pbuf[0:8, :] = pbuf[tm:tm + 8, :]

    h = _modulate(x_ref[...], mod_ref, 1).astype(BF16)
    p = jnp.dot(h, w_ref[...], preferred_element_type=F32)
    v_ref[...] = p[:, d:2 * d].astype(BF16)
    o_ref[...] = p[:, 2 * d:].astype(BF16)

    gg = jnp.dot(h, wgate_ref[...], preferred_element_type=F32) + gbias_ref[...]
    ig = gg[:, :LANES]
    fp = gg[:, LANES:]
    logf = jnp.minimum(fp, 0.0) - jnp.log1p(jnp.exp(-jnp.abs(fp)))
    ri = lax.broadcasted_iota(jnp.int32, (L, L), 0)
    ci = lax.broadcasted_iota(jnp.int32, (L, L), 1)
    tril = jnp.where(ri >= ci, 1.0, 0.0).astype(F32)
    b = jnp.concatenate(
        [jnp.dot(tril, logf[c * L:(c + 1) * L, :], precision=lax.Precision.HIGHEST, preferred_element_type=F32)
         for c in range(tm // L)], axis=0)
    g = ig - b
    row_in_chunk = lax.broadcasted_iota(jnp.int32, (tm, LANES), 0) & (L - 1)
    cm = g
    shift = 1
    while shift < L:
        cm = jnp.maximum(cm, jnp.where(row_in_chunk >= shift, pltpu.roll(cm, shift, 0), -jnp.inf))
        shift *= 2
    bc_ref[...] = b
    cm_ref[...] = cm
    gt_ref[...] = g.T[:nh, :]

    pbuf[8:tm + 8, :] = p[:, :d]
    conv = convw_ref[0:1, :] * pbuf[5:5 + tm, :]
    for i in range(1, CONV_WIDTH):
        conv = conv + convw_ref[i:i + 1, :] * pbuf[5 + i:5 + i + tm, :]
    qk = conv * jax.nn.sigmoid(conv)
    lane = lax.broadcasted_iota(jnp.int32, (tm, LANES), 1)
    for hd in range(nh):
        pair = qk[:, (hd // 2) * LANES:(hd // 2 + 1) * LANES] * (dk ** -0.5)
        keep = (lane >= (hd % 2) * dk) & (lane < (hd % 2 + 1) * dk)
        q_ref[:, hd * LANES:(hd + 1) * LANES] = jnp.where(keep, pair, 0.0).astype(BF16)
    kt_ref[...] = qk[:, nh * dk:].T.astype(BF16)


def _mlstm_proj(x, mod, w_in, gate_bias, conv_w, tm=512):
    b, sq, d = x.shape
    n_main = 3 * d
    n_gate = 2 * MLSTM_HEADS
    nk = MLSTM_HEADS * MLSTM_DQK
    w = w_in[:, :n_main].astype(BF16)
    nh = MLSTM_HEADS
    pad = ((0, 0), (0, LANES - nh))
    wgate = jnp.concatenate([jnp.pad(w_in[:, n_main:n_main + nh], pad),
                             jnp.pad(w_in[:, n_main + nh:], pad)], axis=1).astype(BF16)
    gb = gate_bias.astype(F32).reshape(1, n_gate)
    gbias = jnp.concatenate([jnp.pad(gb[:, :nh], pad), jnp.pad(gb[:, nh:], pad)], axis=1)
    row = pl.BlockSpec((None, tm, d), lambda i, j: (i, j, 0))
    lrow = pl.BlockSpec((None, tm, LANES), lambda i, j: (i, j, 0))
    act = jax.ShapeDtypeStruct((b, sq, d), BF16)
    gate_tile = jax.ShapeDtypeStruct((b, sq, LANES), F32)
    return pl.pallas_call(
        _mproj_body,
        grid=(b, sq // tm),
        in_specs=[row,
                  pl.BlockSpec((None, 3 * N_SUB, d), lambda i, j: (i, 0, 0)),
                  _resident(w.shape), _resident(wgate.shape), _resident(gbias.shape),
                  _resident((CONV_WIDTH, d))],
        out_specs=[row,
                   pl.BlockSpec((None, nk, tm), lambda i, j: (i, 0, j)),
                   row, row, lrow, lrow,
                   pl.BlockSpec((None, nh, tm), lambda i, j: (i, 0, j))],
        out_shape=[act, jax.ShapeDtypeStruct((b, nk, sq), BF16), act, act, gate_tile, gate_tile,
                   jax.ShapeDtypeStruct((b, nh, sq), F32)],
        scratch_shapes=[pltpu.VMEM((tm + 8, d), F32)],
        compiler_params=_cparams(("parallel", "arbitrary"), 48),
        name="mlstm_proj",
    )(x, mod, w, wgate, gbias, conv_w.astype(F32))


def _cell_body(q_ref, kt_ref, v_ref, o_ref, bc_ref, cm_ref, gt_ref, gain_ref, y_ref, cst, mst):
    L = q_ref.shape[0]
    nh, dk, dv = MLSTM_HEADS, MLSTM_DQK, MLSTM_DV
    heads = range(nh)
    c = pl.program_id(1)

    @pl.when(c == 0)
    def _():
        cst[...] = jnp.zeros(cst.shape, F32)
        mst[...] = jnp.zeros(mst.shape, F32)

    m_prev = mst[...]
    bt = bc_ref[...]
    cmt = cm_ref[...]
    mt = jnp.maximum(m_prev, cmt)
    iwt = jnp.exp(m_prev - mt)
    emt = jnp.exp(-(bt + mt))
    b_last = bt[L - 1:L, :]
    cm_last = cmt[L - 1:L, :]
    m_loc = b_last + cm_last
    m_new = jnp.maximum(b_last + m_prev, m_loc)
    sp = jnp.exp(b_last + m_prev - m_new)
    sl = jnp.exp(m_loc - m_new)

    ri = lax.broadcasted_iota(jnp.int32, (L, L), 0)
    ci = lax.broadcasted_iota(jnp.int32, (L, L), 1)
    causal = ri >= ci
    g_row = [gt_ref[h:h + 1, :] for h in heads]
    m_col = [mt[:, h:h + 1] for h in heads]
    dmat = [jnp.exp(jnp.where(causal, g_row[h] - m_col[h], -jnp.inf)) for h in heads]
    wa = [jnp.exp(g_row[h] - cm_last[:, h:h + 1]) for h in heads]

    ones = jnp.ones((L, LANES), BF16)
    q_h = [q_ref[:, h * LANES:(h + 1) * LANES] for h in heads]
    kt_p = [kt_ref[p * LANES:(p + 1) * LANES, :] for p in range(nh // 2)]
    v_aug = [jnp.concatenate([v_ref[:, h * dv:(h + 1) * dv], ones], axis=1) for h in heads]
    c_pair = [jnp.concatenate([cst[2 * p], cst[2 * p + 1]], axis=0).astype(BF16) for p in range(nh // 2)]

    scores = [jnp.dot(q_h[h], kt_p[h // 2], preferred_element_type=F32) for h in heads]
    inter = [jnp.dot(q_h[h], c_pair[h // 2], preferred_element_type=F32) for h in heads]
    pm = [(scores[h] * dmat[h]).astype(BF16) for h in heads]
    iw = [iwt[:, h:h + 1] for h in heads]
    num_aug = [jnp.dot(pm[h], v_aug[h], preferred_element_type=F32) + iw[h] * inter[h] for h in heads]

    em = [emt[:, h:h + 1] for h in heads]
    hh = [num_aug[h][:, :dv] / jnp.maximum(jnp.abs(num_aug[h][:, dv:]), em[h]) for h in heads]
    ms = [jnp.mean(hh[h] * hh[h], axis=-1, keepdims=True) for h in heads]
    rs = [lax.rsqrt(ms[h] + RMS_EPS) for h in heads]
    hn = [hh[h] * rs[h] * gain_ref[:, h * dv:(h + 1) * dv] for h in heads]
    og = [o_ref[:, h * dv:(h + 1) * dv].astype(F32) for h in heads]
    y_ref[...] = jnp.concatenate([(jax.nn.sigmoid(og[h]) * hn[h]).astype(BF16) for h in heads], axis=1)

    kw = [(kt_p[h // 2][(h % 2) * dk:(h % 2 + 1) * dk, :].astype(F32) * wa[h]).astype(BF16) for h in heads]
    c_loc = [jnp.dot(kw[h], v_aug[h], preferred_element_type=F32) for h in heads]
    for h in heads:
        cst[h] = sp[:, h:h + 1] * cst[h] + sl[:, h:h + 1] * c_loc[h]
    mst[...] = m_new


def _mlstm_cell(q, kt, v, o, bc, cm, gt, head_gain):
    b, sq, d = v.shape
    L = CELL_CHUNK
    nk = kt.shape[1]
    row = pl.BlockSpec((None, L, d), lambda i, j: (i, j, 0))
    lrow = pl.BlockSpec((None, L, LANES), lambda i, j: (i, j, 0))
    return pl.pallas_call(
        _cell_body,
        grid=(b, sq // L),
        in_specs=[row,
                  pl.BlockSpec((None, nk, L), lambda i, j: (i, 0, j)),
                  row, row, lrow, lrow,
                  pl.BlockSpec((None, MLSTM_HEADS, L), lambda i, j: (i, 0, j)),
                  _resident((1, d))],
        out_specs=row,
        out_shape=jax.ShapeDtypeStruct((b, sq, d), BF16),
        scratch_shapes=[pltpu.VMEM((MLSTM_HEADS, MLSTM_DQK, 2 * MLSTM_DV), F32),
                        pltpu.VMEM((1, LANES), F32)],
        compiler_params=_cparams(("parallel", "arbitrary"), 32),
        name="mlstm_cell",
    )(q, kt, v, o, bc, cm, gt, head_gain.astype(F32).reshape(1, d))


def _mixer_out_body(y_ref, x_ref, mod_ref, lng_ref, lnb_ref, w_ref, o_ref):
    out = jnp.dot(y_ref[...], w_ref[...], preferred_element_type=F32)
    o_ref[...] = _post(x_ref[...], out, mod_ref, 1, 1.0, lng_ref, lnb_ref)


def _mixer_out(y, x, mod, lng, lnb, w_out, tm=512):
    b, sq, d = x.shape
    w = w_out.astype(BF16)
    row = pl.BlockSpec((None, tm, d), lambda i, j: (i, j, 0))
    return pl.pallas_call(
        _mixer_out_body,
        grid=(b, sq // tm),
        in_specs=[row, row,
                  pl.BlockSpec((None, 3 * N_SUB, d), lambda i, j: (i, 0, 0)),
                  _resident((1, d)), _resident((1, d)), _resident(w.shape)],
        out_specs=row,
        out_shape=jax.ShapeDtypeStruct(x.shape, F32),
        compiler_params=_cparams(("parallel", "parallel"), 32),
        name="mlstm_out",
    )(y, x, mod, lng.reshape(1, d), lnb.reshape(1, d), w)


def _aproj_body(x_ref, mod_ref, w_ref, o_ref):
    h = _modulate(x_ref[...], mod_ref, 1).astype(BF16)
    o_ref[...] = jnp.dot(h, w_ref[...], preferred_element_type=F32).astype(BF16)


def _attn_proj(x, mod, w_bf16, tm=512, tn=3072):
    b, sq, d = x.shape
    n = w_bf16.shape[1]
    return pl.pallas_call(
        _aproj_body,
        grid=(n // tn, b, sq // tm),
        in_specs=[pl.BlockSpec((None, tm, d), lambda g, i, j: (i, j, 0)),
                  pl.BlockSpec((None, 3 * N_SUB, d), lambda g, i, j: (i, 0, 0)),
                  pl.BlockSpec((d, tn), lambda g, i, j: (0, g))],
        out_specs=pl.BlockSpec((None, tm, tn), lambda g, i, j: (i, j, g)),
        out_shape=jax.ShapeDtypeStruct((b, sq, n), BF16),
        compiler_params=_cparams(("arbitrary", "arbitrary", "arbitrary"), 48),
        name="attn_proj",
    )(x, mod, w_bf16)


def _flat_rows(ref):
    v = ref[...]
    return v.reshape(-1, v.shape[-1])


def _attn_body(*refs, has_prev):
    if has_prev:
        q_ref, kc_ref, vc_ref, kp_ref, vp_ref, bias_ref, o_ref, lse_ref = refs
    else:
        q_ref, kc_ref, vc_ref, bias_ref, o_ref, lse_ref = refs
    dh = ATTN_HEAD_DIM
    heads = range(ATTN_HEADS)
    cols = [slice(h * dh, (h + 1) * dh) for h in heads]
    q = _flat_rows(q_ref)
    kc = _flat_rows(kc_ref)
    vc = _flat_rows(vc_ref)
    qb = q.shape[0]
    pb = 0
    if has_prev:
        kp = _flat_rows(kp_ref)
        vp = _flat_rows(vp_ref)
        pb = kp.shape[0]
    bias = bias_ref[...]
    lane = lax.broadcasted_iota(jnp.int32, (qb, LANES), 1)
    nt = (((1,), (1,)), ((), ()))

    scores = []
    for h in heads:
        s = lax.dot_general(q[:, cols[h]], kc[:, cols[h]], nt, preferred_element_type=F32)
        if has_prev:
            sp = lax.dot_general(q[:, cols[h]], kp[:, cols[h]], nt, preferred_element_type=F32)
            s = jnp.concatenate([sp, s], axis=1)
        scores.append(s)
    probs, maxes = [], []
    for h in heads:
        s = scores[h] * (dh ** -0.5) + bias
        mx = jnp.max(s, axis=1, keepdims=True)
        probs.append(jnp.exp(s - mx).astype(BF16))
        maxes.append(mx)
    accs = []
    for h in heads:
        acc = jnp.dot(probs[h][:, pb:], jnp.concatenate([vc[:, cols[h]], jnp.ones((qb, LANES), BF16)], axis=1),
                      preferred_element_type=F32)
        if has_prev:
            acc = acc + jnp.dot(probs[h][:, :pb],
                                jnp.concatenate([vp[:, cols[h]], jnp.ones((pb, LANES), BF16)], axis=1),
                                preferred_element_type=F32)
        accs.append(acc)
    outs = [(accs[h][:, :dh] / accs[h][:, dh:]).astype(BF16) for h in heads]
    lses = [maxes[h] + jnp.log(accs[h][:, dh:]) for h in heads]
    lse_tile = lses[0]
    for h in heads[1:]:
        lse_tile = jnp.where(lane == h, lses[h], lse_tile)
    o_ref[...] = jnp.concatenate(outs, axis=1).reshape(o_ref.shape)
    lse_ref[...] = lse_tile.reshape(lse_ref.shape)


def _window_bias(nslab, cur_rows, prev_rows):
    qb = nslab * cur_rows
    pos_c = np.array([nslab * (i % cur_rows) + i // cur_rows for i in range(qb)])
    pb = nslab * prev_rows
    pos_p = np.array([nslab * (i % prev_rows) + i // prev_rows for i in range(pb)], dtype=np.int64) - pb
    kpos = np.concatenate([pos_p, pos_c])
    delta = pos_c[:, None] - kpos[None, :]
    ok = (delta >= 0) & (delta <= ATT_SPAN)
    later = np.where(ok, 0.0, -np.inf).astype(np.float32)
    first = later.copy()
    first[:, :pb] = -np.inf
    return jnp.asarray(np.stack([first, later]))


def _attn(proj, g, dil, d):
    b, ncls, u, _ = proj.shape
    qb = ATT_QB
    cq, ck, cv = 3 * g, 3 * g + 1, 3 * g + 2
    if dil == NCLS:
        assert u == qb
        bias = _window_bias(1, qb, 0)
        arrs = (proj,) * 3
        def blk(col):
            return pl.BlockSpec((None, None, qb, d), lambda i, r: (i, r, 0, col))
        in_specs = [blk(cq), blk(ck), blk(cv)]
        o_spec = pl.BlockSpec((None, None, qb, d), lambda i, r: (i, r, 0, 0))
        l_spec = pl.BlockSpec((None, None, qb, LANES), lambda i, r: (i, r, 0, 0))
        bias_spec = pl.BlockSpec((None, qb, qb), lambda i, r: (1, 0, 0))
        grid = (b, ncls)
        o_shape, l_shape = (b, ncls, u, d), (b, ncls, u, LANES)
        sem = ("parallel", "parallel")
        has_prev = False
    elif dil == 1:
        ur = qb // ncls
        bias = _window_bias(ncls, ur, ur)
        arrs = (proj,) * 5
        def blk(col, prev):
            if prev:
                return pl.BlockSpec((None, ncls, ur, d), lambda i, n: (i, 0, jnp.maximum(n - 1, 0), col))
            return pl.BlockSpec((None, ncls, ur, d), lambda i, n: (i, 0, n, col))
        in_specs = [blk(cq, False), blk(ck, False), blk(cv, False), blk(ck, True), blk(cv, True)]
        o_spec = pl.BlockSpec((None, ncls, ur, d), lambda i, n: (i, 0, n, 0))
        l_spec = pl.BlockSpec((None, ncls, ur, LANES), lambda i, n: (i, 0, n, 0))
        bias_spec = pl.BlockSpec((None, qb, 2 * qb), lambda i, n: (jnp.minimum(n, 1), 0, 0))
        grid = (b, u // ur)
        o_shape, l_shape = (b, ncls, u, d), (b, ncls, u, LANES)
        sem = ("parallel", "arbitrary")
        has_prev = True
    else:
        nslab = ncls // dil
        ur, urp = qb // nslab, ATT_SPAN // nslab
        bias = _window_bias(nslab, ur, urp)
        p5 = proj.reshape(b, nslab, dil, u, proj.shape[-1])
        arrs = (p5,) * 5
        def blk(col, prev):
            if prev:
                return pl.BlockSpec((None, nslab, None, urp, d),
                                    lambda i, r, n: (i, 0, r, jnp.maximum((ur // urp) * n - 1, 0), col))
            return pl.BlockSpec((None, nslab, None, ur, d), lambda i, r, n: (i, 0, r, n, col))
        in_specs = [blk(cq, False), blk(ck, False), blk(cv, False), blk(ck, True), blk(cv, True)]
        o_spec = pl.BlockSpec((None, nslab, None, ur, d), lambda i, r, n: (i, 0, r, n, 0))
        l_spec = pl.BlockSpec((None, nslab, None, ur, LANES), lambda i, r, n: (i, 0, r, n, 0))
        bias_spec = pl.BlockSpec((None, qb, nslab * urp + qb), lambda i, r, n: (jnp.minimum(n, 1), 0, 0))
        grid = (b, dil, u // ur)
        o_shape, l_shape = (b, nslab, dil, u, d), (b, nslab, dil, u, LANES)
        sem = ("parallel", "parallel", "arbitrary")
        has_prev = True
    o, lse = pl.pallas_call(
        functools.partial(_attn_body, has_prev=has_prev),
        grid=grid,
        in_specs=in_specs + [bias_spec],
        out_specs=[o_spec, l_spec],
        out_shape=[jax.ShapeDtypeStruct(o_shape, BF16), jax.ShapeDtypeStruct(l_shape, F32)],
        compiler_params=_cparams(sem, 40),
        name=f"attn_d{dil}",
    )(*arrs, bias)
    return o.reshape(b, ncls * u, d), lse.reshape(b, ncls * u, LANES)


def _amerge_body(o1_ref, o4_ref, o16_ref, l1_ref, l4_ref, l16_ref, x_ref, mod_ref, lng_ref, lnb_ref,
                 w_ref, out_ref):
    dh = ATTN_HEAD_DIM
    o_refs = (o1_ref, o4_ref, o16_ref)
    lses = [r[...] for r in (l1_ref, l4_ref, l16_ref)]
    heads = range(ATTN_HEADS)
    wts = []
    for h in heads:
        ls = [t[:, h:h + 1] for t in lses]
        mx = jnp.maximum(jnp.maximum(ls[0], ls[1]), ls[2])
        ws = [jnp.exp(t - mx) for t in ls]
        tot = ws[0] + ws[1] + ws[2]
        wts.append([w / tot for w in ws])
    ys = []
    for h in heads:
        cs = slice(h * dh, (h + 1) * dh)
        acc = wts[h][0] * o_refs[0][:, cs].astype(F32)
        for g in (1, 2):
            acc = acc + wts[h][g] * o_refs[g][:, cs].astype(F32)
        ys.append(acc.astype(BF16))
    out = jnp.dot(jnp.concatenate(ys, axis=1), w_ref[...], preferred_element_type=F32)
    out_ref[...] = _post(x_ref[...], out, mod_ref, 1, 1.0, lng_ref, lnb_ref)


def _attn_merge_out(o_list, lse_list, x, mod, lng, lnb, w_out, tm=256):
    b, sq, d = x.shape
    w = w_out.astype(BF16)
    row = pl.BlockSpec((None, tm, d), lambda i, j: (i, j, 0))
    lrow = pl.BlockSpec((None, tm, LANES), lambda i, j: (i, j, 0))
    return pl.pallas_call(
        _amerge_body,
        grid=(b, sq // tm),
        in_specs=[row, row, row, lrow, lrow, lrow, row,
                  pl.BlockSpec((None, 3 * N_SUB, d), lambda i, j: (i, 0, 0)),
                  _resident((1, d)), _resident((1, d)), _resident(w.shape)],
        out_specs=row,
        out_shape=jax.ShapeDtypeStruct(x.shape, F32),
        compiler_params=_cparams(("parallel", "parallel"), 32),
        name="attn_merge_out",
    )(*o_list, *lse_list, x, mod, lng.reshape(1, d), lnb.reshape(1, d), w)


def kernel(x, c, ada_w, ada_b, ln_g, ln_b, ffn_w_in, ffn_w_out, mlstm_w_in, mlstm_gate_bias, mlstm_conv_w,
           mlstm_head_gain, mlstm_w_out, attn_w_in, attn_w_out):
    b, sq, d = x.shape
    assert all(w // dl == ATT_SPAN for w, dl in DIL_GROUPS) and [dl for _, dl in DIL_GROUPS] == [1, 4, NCLS]
    assert sq == NCLS * ATT_QB and d == MLSTM_HEADS * MLSTM_DV == ATTN_HEADS * ATTN_HEAD_DIM
    mod_all = _ada(c, ada_w, ada_b)
    for layer in range(DEPTH):
        mod = mod_all[layer].reshape(b, 3 * N_SUB, d)
        j = layer // 2
        if layer % 2 == 0:
            x = _ffn(x, mod, ln_g[layer, 0], ln_b[layer, 0], ffn_w_in[layer, 0], ffn_w_out[layer, 0], 0)
            q, kt, v, o, bc, cm, gt = _mlstm_proj(x, mod, mlstm_w_in[j], mlstm_gate_bias[j], mlstm_conv_w[j])
            y = _mlstm_cell(q, kt, v, o, bc, cm, gt, mlstm_head_gain[j])
            x = _mixer_out(y, x, mod, ln_g[layer, 1], ln_b[layer, 1], mlstm_w_out[j])
            x = _ffn(x, mod, ln_g[layer, 2], ln_b[layer, 2], ffn_w_in[layer, 1], ffn_w_out[layer, 1], 2)
        else:
            xc = _ffn(x, mod, ln_g[layer, 0], ln_b[layer, 0], ffn_w_in[layer, 0], ffn_w_out[layer, 0], 0,
                      order_out="cls")
            xf = xc.reshape(b, sq, d)
            proj = _attn_proj(xf, mod, attn_w_in[j].astype(BF16)).reshape(b, NCLS, sq // NCLS, -1)
            outs, lses = [], []
            for g, (_, dil) in enumerate(DIL_GROUPS):
                o_g, lse_g = _attn(proj, g, dil, d)
                outs.append(o_g)
                lses.append(lse_g)
            xf = _attn_merge_out(outs, lses, xf, mod, ln_g[layer, 1], ln_b[layer, 1], attn_w_out[j])
            x = _ffn(xf.reshape(b, NCLS, sq // NCLS, d), mod, ln_g[layer, 2], ln_b[layer, 2],
                     ffn_w_in[layer, 1], ffn_w_out[layer, 1], 2, order_in="cls")
    return x
```

```python
import functools

import numpy as np
import jax
import jax.numpy as jnp
from jax import lax
from jax.experimental import pallas as pl
from jax.experimental.pallas import tpu as pltpu

F32 = jnp.float32
BF16 = jnp.bfloat16

DEPTH = 2
N_SUB = 3
D_FF = 2816
MLSTM_HEADS = 8
MLSTM_DQK = 64
MLSTM_DV = 128
CONV_WIDTH = 4
DIL_GROUPS = ((128, 1), (512, 4), (2048, 16))
ATTN_HEADS = 8
ATTN_HEAD_DIM = 128
ALPHA = (2 * DEPTH) ** 0.25
LN_EPS = 1e-5
RMS_EPS = 1e-6
LOG2E = 1.4426950408889634
LN2 = 0.6931471805599453

LANES = 128
NCLS = 16
CELL_CHUNK = 128
CELL_CHUNKS_PER_STEP = 2
ATT_SPAN = 128
ATT_QB = 256
ATT_BLOCKS_PER_STEP = 2
MIB = 1024 * 1024


def _cparams(sem, vmem_mib):
    return pltpu.CompilerParams(dimension_semantics=sem, vmem_limit_bytes=int(vmem_mib * MIB))


def _resident(shape):
    nd = len(shape)
    return pl.BlockSpec(shape, lambda *_: (0,) * nd, pipeline_mode=pl.Buffered(1))


def _layer_norm(y, g, b):
    mu = jnp.mean(y, axis=-1, keepdims=True)
    yc = y - mu
    var = jnp.mean(yc * yc, axis=-1, keepdims=True)
    return yc * lax.rsqrt(var + LN_EPS) * g + b


def _modulate(x, mod_ref, s):
    return x * (1.0 + mod_ref[3 * s + 1:3 * s + 2, :]) + mod_ref[3 * s:3 * s + 1, :]


def _post(x, out, mod_ref, s, weight, lng_ref, lnb_ref):
    y = ALPHA * x + (weight * (1.0 + mod_ref[3 * s + 2:3 * s + 3, :])) * out
    return _layer_norm(y, lng_ref[...], lnb_ref[...])


def _ada_body(c_ref, w_ref, b_ref, o_ref):
    c = c_ref[...]
    cond = c * jax.nn.sigmoid(c)
    o_ref[...] = jnp.dot(cond, w_ref[...], precision=lax.Precision.HIGHEST,
                         preferred_element_type=F32) + b_ref[...]


def _ada(c, ada_w, ada_b):
    depth, d, n = ada_w.shape
    b = c.shape[0]
    tn = 1152
    return pl.pallas_call(
        _ada_body,
        grid=(depth, n // tn),
        in_specs=[pl.BlockSpec((b, d), lambda l, j: (0, 0)),
                  pl.BlockSpec((None, d, tn), lambda l, j: (l, 0, j)),
                  pl.BlockSpec((None, 1, tn), lambda l, j: (l, 0, j))],
        out_specs=pl.BlockSpec((None, b, tn), lambda l, j: (l, 0, j)),
        out_shape=jax.ShapeDtypeStruct((depth, b, n), F32),
        compiler_params=_cparams(("arbitrary", "arbitrary"), 32),
        name="ada_mod",
    )(c, ada_w, ada_b.reshape(depth, 1, n))


def _ffn_body(x_ref, mod_ref, lng_ref, lnb_ref, wg_ref, wu_ref, wo_ref, o_ref, *, s, order_in, order_out):
    if order_in == "cls":
        tm = x_ref.shape[0] * x_ref.shape[1]
        x = x_ref[...].reshape(tm, x_ref.shape[2])
    elif order_out == "cls":
        tm, d = x_ref.shape
        x = jnp.swapaxes(x_ref[...].reshape(tm // NCLS, NCLS, d), 0, 1).reshape(tm, d)
    else:
        tm = x_ref.shape[0]
        x = x_ref[...]
    h = _modulate(x, mod_ref, s).astype(BF16)
    g = jnp.dot(h, wg_ref[...], preferred_element_type=F32)
    u = jnp.dot(h, wu_ref[...], preferred_element_type=F32)
    a = (g * jax.nn.sigmoid(g) * u).astype(BF16)
    out = jnp.dot(a, wo_ref[...], preferred_element_type=F32)
    y = _post(x, out, mod_ref, s, 0.5, lng_ref, lnb_ref)
    if order_out == "cls":
        o_ref[...] = y.reshape(o_ref.shape)
    elif order_in == "cls":
        o_ref[...] = jnp.swapaxes(y.reshape(NCLS, tm // NCLS, y.shape[1]), 0, 1).reshape(o_ref.shape)
    else:
        o_ref[...] = y


def _ffn(x, mod, lng, lnb, w_in, w_out, s, order_in="nat", order_out="nat", tm=512):
    d = x.shape[-1]
    b = x.shape[0]
    sq = x.shape[1] if order_in == "nat" else x.shape[1] * x.shape[2]
    wg = w_in[:, :D_FF].astype(BF16)
    wu = w_in[:, D_FF:].astype(BF16)
    wo = w_out.astype(BF16)
    nat = pl.BlockSpec((None, tm, d), lambda i, j: (i, j, 0))
    cls = pl.BlockSpec((None, NCLS, tm // NCLS, d), lambda i, j: (i, 0, j, 0))
    out_shape = (b, sq, d) if order_out == "nat" else (b, NCLS, sq // NCLS, d)
    return pl.pallas_call(
        functools.partial(_ffn_body, s=s, order_in=order_in, order_out=order_out),
        grid=(b, sq // tm),
        in_specs=[cls if order_in == "cls" else nat,
                  pl.BlockSpec((None, 3 * N_SUB, d), lambda i, j: (i, 0, 0)),
                  _resident((1, d)), _resident((1, d)),
                  _resident(wg.shape), _resident(wu.shape), _resident(wo.shape)],
        out_specs=cls if order_out == "cls" else nat,
        out_shape=jax.ShapeDtypeStruct(out_shape, F32),
        compiler_params=_cparams(("parallel", "parallel"), 56),
        name=f"ffn_s{s}_{order_in}_{order_out}",
    )(x, mod, lng.reshape(1, d), lnb.reshape(1, d), wg, wu, wo)


def _mproj_body(x_ref, mod_ref, w_ref, wgate_ref, gbias_ref, convw_ref,
                q_ref, kt_ref, v_ref, o_ref, bc_ref, cm_ref, gt_ref, pbuf):
    tm, d = x_ref.shape
    nh, dk = MLSTM_HEADS, MLSTM_DQK
    L = CELL_CHUNK
    j = pl.program_id(1)

    @pl.when(j == 0)
    def _():
        pbuf[0:8, :] = jnp.zeros((8, d), F32)

    @pl.when(j > 0)
    def _():
        pbuf[0:8, :] = pbuf[tm:tm + 8, :]

    h = _modulate(x_ref[...], mod_ref, 1).astype(BF16)
    p_qk = jnp.dot(h, w_ref[:, :d], preferred_element_type=F32)
    gg = jnp.dot(h, wgate_ref[...], preferred_element_type=F32) + gbias_ref[...]
    v_ref[...] = jnp.dot(h, w_ref[:, d:2 * d], preferred_element_type=F32).astype(BF16)

    pbuf[8:tm + 8, :] = p_qk
    last = CONV_WIDTH - 1
    conv = convw_ref[last:last + 1, :] * pbuf[8:8 + tm, :]
    for i in range(last - 1, -1, -1):
        conv = conv + convw_ref[i:i + 1, :] * pbuf[8 - last + i:8 - last + i + tm, :]
    qk = conv * jax.nn.sigmoid(conv)
    lane = lax.broadcasted_iota(jnp.int32, (tm, LANES), 1)
    for hd in range(nh):
        pair = qk[:, (hd // 2) * LANES:(hd // 2 + 1) * LANES] * (dk ** -0.5)
        keep = (lane >= (hd % 2) * dk) & (lane < (hd % 2 + 1) * dk)
        q_ref[:, hd * LANES:(hd + 1) * LANES] = jnp.where(keep, pair, 0.0).astype(BF16)
    kt_ref[...] = qk[:, nh * dk:].T.astype(BF16)

    o_ref[...] = jnp.dot(h, w_ref[:, 2 * d:], preferred_element_type=F32).astype(BF16)

    ig = gg[:, :LANES]
    fp = gg[:, LANES:]
    logf = jnp.minimum(fp, 0.0) - jnp.log1p(jnp.exp(-jnp.abs(fp)))
    ri = lax.broadcasted_iota(jnp.int32, (L, L), 0)
    ci = lax.broadcasted_iota(jnp.int32, (L, L), 1)
    tril = jnp.where(ri >= ci, 1.0, 0.0).astype(F32)
    b = jnp.concatenate(
        [jnp.dot(tril, logf[c * L:(c + 1) * L, :], precision=lax.Precision.HIGHEST, preferred_element_type=F32)
         for c in range(tm // L)], axis=0)
    g = ig - b
    row_in_chunk = lax.broadcasted_iota(jnp.int32, (tm, LANES), 0) & (L - 1)
    cm = g
    shift = 1
    while shift < L:
        cm = jnp.maximum(cm, jnp.where(row_in_chunk >= shift, pltpu.roll(cm, shift, 0), -jnp.inf))
        shift *= 2
    bc_ref[...] = b
    cm_ref[...] = cm
    gt_ref[...] = g.T[:nh, :]


def _mlstm_proj(x, mod, w_in, gate_bias, conv_w, tm=512):
    b, sq, d = x.shape
    n_main = 3 * d
    n_gate = 2 * MLSTM_HEADS
    nk = MLSTM_HEADS * MLSTM_DQK
    w = w_in[:, :n_main].astype(BF16)
    nh = MLSTM_HEADS
    pad = ((0, 0), (0, LANES - nh))
    wgate = jnp.concatenate([jnp.pad(w_in[:, n_main:n_main + nh], pad),
                             jnp.pad(w_in[:, n_main + nh:], pad)], axis=1).astype(BF16)
    gb = gate_bias.astype(F32).reshape(1, n_gate)
    gbias = jnp.concatenate([jnp.pad(gb[:, :nh], pad), jnp.pad(gb[:, nh:], pad)], axis=1)
    row = pl.BlockSpec((None, tm, d), lambda i, j: (i, j, 0))
    lrow = pl.BlockSpec((None, tm, LANES), lambda i, j: (i, j, 0))
    act = jax.ShapeDtypeStruct((b, sq, d), BF16)
    gate_tile = jax.ShapeDtypeStruct((b, sq, LANES), F32)
    return pl.pallas_call(
        _mproj_body,
        grid=(b, sq // tm),
        in_specs=[row,
                  pl.BlockSpec((None, 3 * N_SUB, d), lambda i, j: (i, 0, 0)),
                  _resident(w.shape), _resident(wgate.shape), _resident(gbias.shape),
                  _resident((CONV_WIDTH, d))],
        out_specs=[row,
                   pl.BlockSpec((None, nk, tm), lambda i, j: (i, 0, j)),
                   row, row, lrow, lrow,
                   pl.BlockSpec((None, nh, tm), lambda i, j: (i, 0, j))],
        out_shape=[act, jax.ShapeDtypeStruct((b, nk, sq), BF16), act, act, gate_tile, gate_tile,
                   jax.ShapeDtypeStruct((b, nh, sq), F32)],
        scratch_shapes=[pltpu.VMEM((tm + 8, d), F32)],
        compiler_params=_cparams(("parallel", "arbitrary"), 48),
        name="mlstm_proj",
    )(x, mod, w, wgate, gbias, conv_w.astype(F32))


def _cell_body(q_ref, kt_ref, v_ref, o_ref, bc_ref, cm_ref, gt_ref, gain_ref, y_ref, cst, mst):
    @pl.when(pl.program_id(1) == 0)
    def _():
        cst[...] = jnp.zeros(cst.shape, F32)
        mst[...] = jnp.zeros(mst.shape, F32)

    for s in range(q_ref.shape[0] // CELL_CHUNK):
        _cell_chunk(s, q_ref, kt_ref, v_ref, o_ref, bc_ref, cm_ref, gt_ref, gain_ref, y_ref, cst, mst)


def _cell_chunk(s, q_ref, kt_ref, v_ref, o_ref, bc_ref, cm_ref, gt_ref, gain_ref, y_ref, cst, mst):
    L = CELL_CHUNK
    rows = slice(s * L, (s + 1) * L)
    nh, dk, dv = MLSTM_HEADS, MLSTM_DQK, MLSTM_DV
    heads = range(nh)

    m_prev = mst[...]
    bt = bc_ref[rows, :]
    cmt = cm_ref[rows, :]
    mt = jnp.maximum(m_prev, cmt)
    iwt = jnp.exp(m_prev - mt)
    emt = jnp.exp(-(bt + mt))
    b_last = bt[L - 1:L, :]
    cm_last = cmt[L - 1:L, :]
    m_loc = b_last + cm_last
    m_new = jnp.maximum(b_last + m_prev, m_loc)
    sp = jnp.exp(b_last + m_prev - m_new)
    sl = jnp.exp(m_loc - m_new)

    ri = lax.broadcasted_iota(jnp.int32, (L, L), 0)
    ci = lax.broadcasted_iota(jnp.int32, (L, L), 1)
    causal = ri >= ci
    g_row = [gt_ref[h:h + 1, rows] for h in heads]
    m_col = [mt[:, h:h + 1] for h in heads]
    dmat = [jnp.exp(jnp.where(causal, g_row[h] - m_col[h], -jnp.inf)) for h in heads]
    wa = [jnp.exp(g_row[h] - cm_last[:, h:h + 1]) for h in heads]

    ones = jnp.ones((L, LANES), BF16)
    q_h = [q_ref[rows, h * LANES:(h + 1) * LANES] for h in heads]
    kt_p = [kt_ref[p * LANES:(p + 1) * LANES, rows] for p in range(nh // 2)]
    v_aug = [jnp.concatenate([v_ref[rows, h * dv:(h + 1) * dv], ones], axis=1) for h in heads]
    c_pair = [jnp.concatenate([cst[2 * p], cst[2 * p + 1]], axis=0).astype(BF16) for p in range(nh // 2)]

    scores = [jnp.dot(q_h[h], kt_p[h // 2], preferred_element_type=F32) for h in heads]
    inter = [jnp.dot(q_h[h], c_pair[h // 2], preferred_element_type=F32) for h in heads]
    pm = [(scores[h] * dmat[h]).astype(BF16) for h in heads]
    iw = [iwt[:, h:h + 1] for h in heads]
    num_aug = [jnp.dot(pm[h], v_aug[h], preferred_element_type=F32) + iw[h] * inter[h] for h in heads]

    em = [emt[:, h:h + 1] for h in heads]
    hh = [num_aug[h][:, :dv] / jnp.maximum(jnp.abs(num_aug[h][:, dv:]), em[h]) for h in heads]
    ms = [jnp.mean(hh[h] * hh[h], axis=-1, keepdims=True) for h in heads]
    rs = [lax.rsqrt(ms[h] + RMS_EPS) for h in heads]
    hn = [hh[h] * rs[h] * gain_ref[:, h * dv:(h + 1) * dv] for h in heads]
    og = [o_ref[rows, h * dv:(h + 1) * dv].astype(F32) for h in heads]
    y_ref[rows, :] = jnp.concatenate([(jax.nn.sigmoid(og[h]) * hn[h]).astype(BF16) for h in heads], axis=1)

    kw = [(kt_p[h // 2][(h % 2) * dk:(h % 2 + 1) * dk, :].astype(F32) * wa[h]).astype(BF16) for h in heads]
    c_loc = [jnp.dot(kw[h], v_aug[h], preferred_element_type=F32) for h in heads]
    for h in heads:
        cst[h] = sp[:, h:h + 1] * cst[h] + sl[:, h:h + 1] * c_loc[h]
    mst[...] = m_new


def _mlstm_cell(q, kt, v, o, bc, cm, gt, head_gain):
    b, sq, d = v.shape
    L = CELL_CHUNKS_PER_STEP * CELL_CHUNK
    nk = kt.shape[1]
    row = pl.BlockSpec((None, L, d), lambda i, j: (i, j, 0))
    lrow = pl.BlockSpec((None, L, LANES), lambda i, j: (i, j, 0))
    return pl.pallas_call(
        _cell_body,
        grid=(b, sq // L),
        in_specs=[row,
                  pl.BlockSpec((None, nk, L), lambda i, j: (i, 0, j)),
                  row, row, lrow, lrow,
                  pl.BlockSpec((None, MLSTM_HEADS, L), lambda i, j: (i, 0, j)),
                  _resident((1, d))],
        out_specs=row,
        out_shape=jax.ShapeDtypeStruct((b, sq, d), BF16),
        scratch_shapes=[pltpu.VMEM((MLSTM_HEADS, MLSTM_DQK, 2 * MLSTM_DV), F32),
                        pltpu.VMEM((1, LANES), F32)],
        compiler_params=_cparams(("parallel", "arbitrary"), 32),
        name="mlstm_cell",
    )(q, kt, v, o, bc, cm, gt, head_gain.astype(F32).reshape(1, d))


def _mixer_out_body(y_ref, x_ref, mod_ref, lng_ref, lnb_ref, w_ref, o_ref):
    out = jnp.dot(y_ref[...], w_ref[...], preferred_element_type=F32)
    o_ref[...] = _post(x_ref[...], out, mod_ref, 1, 1.0, lng_ref, lnb_ref)


def _mixer_out(y, x, mod, lng, lnb, w_out, tm=512):
    b, sq, d = x.shape
    w = w_out.astype(BF16)
    row = pl.BlockSpec((None, tm, d), lambda i, j: (i, j, 0))
    return pl.pallas_call(
        _mixer_out_body,
        grid=(b, sq // tm),
        in_specs=[row, row,
                  pl.BlockSpec((None, 3 * N_SUB, d), lambda i, j: (i, 0, 0)),
                  _resident((1, d)), _resident((1, d)), _resident(w.shape)],
        out_specs=row,
        out_shape=jax.ShapeDtypeStruct(x.shape, F32),
        compiler_params=_cparams(("parallel", "parallel"), 32),
        name="mlstm_out",
    )(y, x, mod, lng.reshape(1, d), lnb.reshape(1, d), w)


def _aproj_body(x_ref, mod_ref, w_ref, o_ref):
    h = _modulate(x_ref[...], mod_ref, 1).astype(BF16)
    o_ref[...] = jnp.dot(h, w_ref[...], preferred_element_type=F32).astype(BF16)


def _attn_proj(x, mod, w_bf16, tm=512, tn=3072):
    b, sq, d = x.shape
    n = w_bf16.shape[1]
    return pl.pallas_call(
        _aproj_body,
        grid=(n // tn, b, sq // tm),
        in_specs=[pl.BlockSpec((None, tm, d), lambda g, i, j: (i, j, 0)),
                  pl.BlockSpec((None, 3 * N_SUB, d), lambda g, i, j: (i, 0, 0)),
                  pl.BlockSpec((d, tn), lambda g, i, j: (0, g))],
        out_specs=pl.BlockSpec((None, tm, tn), lambda g, i, j: (i, j, g)),
        out_shape=jax.ShapeDtypeStruct((b, sq, n), BF16),
        compiler_params=_cparams(("arbitrary", "arbitrary", "arbitrary"), 48),
        name="attn_proj",
    )(x, mod, w_bf16)


def _flat_rows(v):
    return v.reshape(-1, v.shape[-1])


def _attn_block(q, kc, vc, kp, vp, bias):
    has_prev = kp is not None
    dh = ATTN_HEAD_DIM
    heads = range(ATTN_HEADS)
    cols = [slice(h * dh, (h + 1) * dh) for h in heads]
    qb = q.shape[0]
    pb = kp.shape[0] if has_prev else 0
    lane = lax.broadcasted_iota(jnp.int32, (qb, LANES), 1)
    nt = (((1,), (1,)), ((), ()))

    scores = []
    for h in heads:
        s = lax.dot_general(q[:, cols[h]], kc[:, cols[h]], nt, preferred_element_type=F32)
        if has_prev:
            sp = lax.dot_general(q[:, cols[h]], kp[:, cols[h]], nt, preferred_element_type=F32)
            s = jnp.concatenate([sp, s], axis=1)
        scores.append(s)
    probs, maxes = [], []
    for h in heads:
        t = scores[h] * (dh ** -0.5 * LOG2E) + bias
        mx = jnp.max(t, axis=1, keepdims=True)
        probs.append(jnp.exp2(t - mx).astype(BF16))
        maxes.append(mx)
    accs = []
    for h in heads:
        acc = jnp.dot(probs[h][:, pb:], jnp.concatenate([vc[:, cols[h]], jnp.ones((qb, LANES), BF16)], axis=1),
                      preferred_element_type=F32)
        if has_prev:
            acc = acc + jnp.dot(probs[h][:, :pb],
                                jnp.concatenate([vp[:, cols[h]], jnp.ones((pb, LANES), BF16)], axis=1),
                                preferred_element_type=F32)
        accs.append(acc)
    outs = [(accs[h][:, :dh] / accs[h][:, dh:]).astype(BF16) for h in heads]
    lses = [maxes[h] * LN2 + jnp.log(accs[h][:, dh:]) for h in heads]
    lse_tile = lses[0]
    for h in heads[1:]:
        lse_tile = jnp.where(lane == h, lses[h], lse_tile)
    return jnp.concatenate(outs, axis=1), lse_tile


def _attn_whole_body(q_ref, k_ref, v_ref, bias_ref, o_ref, lse_ref):
    for c in range(q_ref.shape[0]):
        o, lse = _attn_block(q_ref[c], k_ref[c], v_ref[c], None, None, bias_ref[1])
        o_ref[c] = o
        lse_ref[c] = lse


def _attn_chain_body(q_ref, k_ref, v_ref, kp_ref, vp_ref, bias_ref, o_ref, lse_ref, *, n_axis):
    n = pl.program_id(n_axis)
    ur = q_ref.shape[1] // 2
    urp = kp_ref.shape[1]

    def part(ref, lo, hi):
        return _flat_rows(ref[:, lo:hi, :])

    o_a, l_a = _attn_block(part(q_ref, 0, ur), part(k_ref, 0, ur), part(v_ref, 0, ur),
                           _flat_rows(kp_ref[...]), _flat_rows(vp_ref[...]), bias_ref[jnp.minimum(n, 1)])
    o_b, l_b = _attn_block(part(q_ref, ur, 2 * ur), part(k_ref, ur, 2 * ur), part(v_ref, ur, 2 * ur),
                           part(k_ref, ur - urp, ur), part(v_ref, ur - urp, ur), bias_ref[1])
    nslab = q_ref.shape[0]
    o_ref[:, 0:ur, :] = o_a.reshape(nslab, ur, o_a.shape[-1])
    o_ref[:, ur:2 * ur, :] = o_b.reshape(nslab, ur, o_b.shape[-1])
    lse_ref[:, 0:ur, :] = l_a.reshape(nslab, ur, LANES)
    lse_ref[:, ur:2 * ur, :] = l_b.reshape(nslab, ur, LANES)


def _window_bias(nslab, cur_rows, prev_rows):
    qb = nslab * cur_rows
    pos_c = np.array([nslab * (i % cur_rows) + i // cur_rows for i in range(qb)])
    pb = nslab * prev_rows
    pos_p = np.array([nslab * (i % prev_rows) + i // prev_rows for i in range(pb)], dtype=np.int64) - pb
    kpos = np.concatenate([pos_p, pos_c])
    delta = pos_c[:, None] - kpos[None, :]
    ok = (delta >= 0) & (delta <= ATT_SPAN)
    later = np.where(ok, 0.0, -np.inf).astype(np.float32)
    first = later.copy()
    first[:, :pb] = -np.inf
    return jnp.asarray(np.stack([first, later]))


def _attn(proj, g, dil, d):
    b, ncls, u, _ = proj.shape
    qb = ATT_QB
    cq, ck, cv = 3 * g, 3 * g + 1, 3 * g + 2
    nsub = ATT_BLOCKS_PER_STEP
    if dil == NCLS:
        assert u == qb
        bias = _window_bias(1, qb, 0)
        arrs = (proj,) * 3
        def blk(col, width=d):
            return pl.BlockSpec((None, nsub, qb, width), lambda i, r: (i, r, 0, col))
        in_specs = [blk(cq), blk(ck), blk(cv)]
        o_spec, l_spec = blk(0), blk(0, LANES)
        grid = (b, ncls // nsub)
        o_shape, l_shape = (b, ncls, u, d), (b, ncls, u, LANES)
        sem = ("parallel", "parallel")
        body = _attn_whole_body
    else:
        nslab = ncls // dil
        ur = qb // nslab
        urp = max(ATT_SPAN // nslab, 16)
        bias = _window_bias(nslab, ur, urp)
        p5 = proj.reshape(b, nslab, dil, u, proj.shape[-1])
        arrs = (p5,) * 5
        step = nsub * ur
        def blk(col, width=d):
            return pl.BlockSpec((None, nslab, None, step, width), lambda i, r, n: (i, 0, r, n, col))
        def prev(col):
            return pl.BlockSpec((None, nslab, None, urp, d),
                                lambda i, r, n: (i, 0, r, jnp.maximum((step // urp) * n - 1, 0), col))
        in_specs = [blk(cq), blk(ck), blk(cv), prev(ck), prev(cv)]
        o_spec, l_spec = blk(0), blk(0, LANES)
        grid = (b, dil, u // step)
        o_shape, l_shape = (b, nslab, dil, u, d), (b, nslab, dil, u, LANES)
        sem = ("parallel", "parallel", "arbitrary")
        body = functools.partial(_attn_chain_body, n_axis=2)
    bias_spec = pl.BlockSpec(bias.shape, lambda *_: (0, 0, 0))
    o, lse = pl.pallas_call(
        body,
        grid=grid,
        in_specs=in_specs + [bias_spec],
        out_specs=[o_spec, l_spec],
        out_shape=[jax.ShapeDtypeStruct(o_shape, BF16), jax.ShapeDtypeStruct(l_shape, F32)],
        compiler_params=_cparams(sem, 40),
        name=f"attn_d{dil}",
    )(*arrs, bias)
    return o.reshape(b, ncls * u, d), lse.reshape(b, ncls * u, LANES)


def _amerge_body(o1_ref, o4_ref, o16_ref, l1_ref, l4_ref, l16_ref, x_ref, mod_ref, lng_ref, lnb_ref,
                 w_ref, out_ref):
    dh = ATTN_HEAD_DIM
    o_refs = (o1_ref, o4_ref, o16_ref)
    lses = [r[...] for r in (l1_ref, l4_ref, l16_ref)]
    heads = range(ATTN_HEADS)
    wts = []
    for h in heads:
        ls = [t[:, h:h + 1] for t in lses]
        mx = jnp.maximum(jnp.maximum(ls[0], ls[1]), ls[2])
        ws = [jnp.exp(t - mx) for t in ls]
        tot = ws[0] + ws[1] + ws[2]
        wts.append([w / tot for w in ws])
    ys = []
    for h in heads:
        cs = slice(h * dh, (h + 1) * dh)
        acc = wts[h][0] * o_refs[0][:, cs].astype(F32)
        for g in (1, 2):
            acc = acc + wts[h][g] * o_refs[g][:, cs].astype(F32)
        ys.append(acc.astype(BF16))
    out = jnp.dot(jnp.concatenate(ys, axis=1), w_ref[...], preferred_element_type=F32)
    out_ref[...] = _post(x_ref[...], out, mod_ref, 1, 1.0, lng_ref, lnb_ref)


def _attn_merge_out(o_list, lse_list, x, mod, lng, lnb, w_out, tm=512):
    b, sq, d = x.shape
    w = w_out.astype(BF16)
    row = pl.BlockSpec((None, tm, d), lambda i, j: (i, j, 0))
    lrow = pl.BlockSpec((None, tm, LANES), lambda i, j: (i, j, 0))
    return pl.pallas_call(
        _amerge_body,
        grid=(b, sq // tm),
        in_specs=[row, row, row, lrow, lrow, lrow, row,
                  pl.BlockSpec((None, 3 * N_SUB, d), lambda i, j: (i, 0, 0)),
                  _resident((1, d)), _resident((1, d)), _resident(w.shape)],
        out_specs=row,
        out_shape=jax.ShapeDtypeStruct(x.shape, F32),
        compiler_params=_cparams(("parallel", "parallel"), 32),
        name="attn_merge_out",
    )(*o_list, *lse_list, x, mod, lng.reshape(1, d), lnb.reshape(1, d), w)


def kernel(x, c, ada_w, ada_b, ln_g, ln_b, ffn_w_in, ffn_w_out, mlstm_w_in, mlstm_gate_bias, mlstm_conv_w,
           mlstm_head_gain, mlstm_w_out, attn_w_in, attn_w_out):
    b, sq, d = x.shape
    assert all(w // dl == ATT_SPAN for w, dl in DIL_GROUPS) and [dl for _, dl in DIL_GROUPS] == [1, 4, NCLS]
    assert sq == NCLS * ATT_QB and d == MLSTM_HEADS * MLSTM_DV == ATTN_HEADS * ATTN_HEAD_DIM
    mod_all = _ada(c, ada_w, ada_b)
    for layer in range(DEPTH):
        mod = mod_all[layer].reshape(b, 3 * N_SUB, d)
        j = layer // 2
        if layer % 2 == 0:
            x = _ffn(x, mod, ln_g[layer, 0], ln_b[layer, 0], ffn_w_in[layer, 0], ffn_w_out[layer, 0], 0)
            q, kt, v, o, bc, cm, gt = _mlstm_proj(x, mod, mlstm_w_in[j], mlstm_gate_bias[j], mlstm_conv_w[j])
            y = _mlstm_cell(q, kt, v, o, bc, cm, gt, mlstm_head_gain[j])
            x = _mixer_out(y, x, mod, ln_g[layer, 1], ln_b[layer, 1], mlstm_w_out[j])
            x = _ffn(x, mod, ln_g[layer, 2], ln_b[layer, 2], ffn_w_in[layer, 1], ffn_w_out[layer, 1], 2)
        else:
            xc = _ffn(x, mod, ln_g[layer, 0], ln_b[layer, 0], ffn_w_in[layer, 0], ffn_w_out[layer, 0], 0,
                      order_out="cls")
            xf = xc.reshape(b, sq, d)
            proj = _attn_proj(xf, mod, attn_w_in[j].astype(BF16)).reshape(b, NCLS, sq // NCLS, -1)
            outs, lses = [], []
            for g, (_, dil) in enumerate(DIL_GROUPS):
                o_g, lse_g = _attn(proj, g, dil, d)
                outs.append(o_g)
                lses.append(lse_g)
            xf = _attn_merge_out(outs, lses, xf, mod, ln_g[layer, 1], ln_b[layer, 1], attn_w_out[j])
            x = _ffn(xf.reshape(b, NCLS, sq // NCLS, d), mod, ln_g[layer, 2], ln_b[layer, 2],
                     ffn_w_in[layer, 1], ffn_w_out[layer, 1], 2, order_in="cls")
    return x
```

```python
import functools

import numpy as np
import jax
import jax.numpy as jnp
from jax import lax
from jax.experimental import pallas as pl
from jax.experimental.pallas import tpu as pltpu

F32 = jnp.float32
BF16 = jnp.bfloat16

DEPTH = 2
N_SUB = 3
D_FF = 2816
MLSTM_HEADS = 8
MLSTM_DQK = 64
MLSTM_DV = 128
CONV_WIDTH = 4
DIL_GROUPS = ((128, 1), (512, 4), (2048, 16))
ATTN_HEADS = 8
ATTN_HEAD_DIM = 128
ALPHA = (2 * DEPTH) ** 0.25
LN_EPS = 1e-5
RMS_EPS = 1e-6
LOG2E = 1.4426950408889634
LN2 = 0.6931471805599453

LANES = 128
NCLS = 16
CELL_CHUNK = 128
CELL_CHUNKS_PER_STEP = 2
ATT_SPAN = 128
ATT_QB = 256
ATT_BLOCKS_PER_STEP = 2
MIB = 1024 * 1024


def _cparams(sem, vmem_mib):
    return pltpu.CompilerParams(dimension_semantics=sem, vmem_limit_bytes=int(vmem_mib * MIB))


def _resident(shape):
    nd = len(shape)
    return pl.BlockSpec(shape, lambda *_: (0,) * nd, pipeline_mode=pl.Buffered(1))


def _layer_norm(y, g, b):
    mu = jnp.mean(y, axis=-1, keepdims=True)
    yc = y - mu
    var = jnp.mean(yc * yc, axis=-1, keepdims=True)
    return yc * lax.rsqrt(var + LN_EPS) * g + b


def _modulate(x, mod_ref, s):
    return x * (1.0 + mod_ref[3 * s + 1:3 * s + 2, :]) + mod_ref[3 * s:3 * s + 1, :]


def _post(x, out, mod_ref, s, weight, lng_ref, lnb_ref):
    y = ALPHA * x + (weight * (1.0 + mod_ref[3 * s + 2:3 * s + 3, :])) * out
    return _layer_norm(y, lng_ref[...], lnb_ref[...])


def _ada_body(c_ref, w_ref, b_ref, o_ref):
    c = c_ref[...]
    cond = c * jax.nn.sigmoid(c)
    o_ref[...] = jnp.dot(cond, w_ref[...], precision=lax.Precision.HIGHEST,
                         preferred_element_type=F32) + b_ref[...]


def _ada(c, ada_w, ada_b):
    depth, d, n = ada_w.shape
    b = c.shape[0]
    tn = 1152
    return pl.pallas_call(
        _ada_body,
        grid=(depth, n // tn),
        in_specs=[pl.BlockSpec((b, d), lambda l, j: (0, 0)),
                  pl.BlockSpec((None, d, tn), lambda l, j: (l, 0, j)),
                  pl.BlockSpec((None, 1, tn), lambda l, j: (l, 0, j))],
        out_specs=pl.BlockSpec((None, b, tn), lambda l, j: (l, 0, j)),
        out_shape=jax.ShapeDtypeStruct((depth, b, n), F32),
        compiler_params=_cparams(("arbitrary", "arbitrary"), 32),
        name="ada_mod",
    )(c, ada_w, ada_b.reshape(depth, 1, n))


def _ffn_body(x_ref, mod_ref, modp_ref, lng_ref, lnb_ref, wg_ref, wu_ref, wo_ref, o_ref, xs_ref, g_ref, u_ref,
              *, s, order_in, order_out):
    @pl.when(pl.program_id(0) == 0)
    def _():
        xs_ref[...] = jnp.zeros(xs_ref.shape, F32)
        g_ref[...] = jnp.zeros(g_ref.shape, F32)
        u_ref[...] = jnp.zeros(u_ref.shape, F32)

    tm = xs_ref.shape[0]
    g = g_ref[...]
    a = (g * jax.nn.sigmoid(g) * u_ref[...]).astype(BF16)
    out = jnp.dot(a, wo_ref[...], preferred_element_type=F32)
    y = _post(xs_ref[...], out, modp_ref, s, 0.5, lng_ref, lnb_ref)
    if order_out == "cls":
        o_ref[...] = y.reshape(o_ref.shape)
    elif order_in == "cls":
        o_ref[...] = jnp.swapaxes(y.reshape(NCLS, tm // NCLS, y.shape[1]), 0, 1).reshape(o_ref.shape)
    else:
        o_ref[...] = y

    if order_in == "cls":
        x = x_ref[...].reshape(tm, x_ref.shape[2])
    elif order_out == "cls":
        d = x_ref.shape[1]
        x = jnp.swapaxes(x_ref[...].reshape(tm // NCLS, NCLS, d), 0, 1).reshape(tm, d)
    else:
        x = x_ref[...]
    h = _modulate(x, mod_ref, s).astype(BF16)
    g_ref[...] = jnp.dot(h, wg_ref[...], preferred_element_type=F32)
    u_ref[...] = jnp.dot(h, wu_ref[...], preferred_element_type=F32)
    xs_ref[...] = x


def _ffn(x, mod, lng, lnb, w_in, w_out, s, order_in="nat", order_out="nat", tm=512):
    d = x.shape[-1]
    b = x.shape[0]
    sq = x.shape[1] if order_in == "nat" else x.shape[1] * x.shape[2]
    wg = w_in[:, :D_FF].astype(BF16)
    wu = w_in[:, D_FF:].astype(BF16)
    wo = w_out.astype(BF16)
    tpb = sq // tm
    ntiles = b * tpb

    def cur(t):
        return jnp.minimum(t, ntiles - 1)

    def prv(t):
        return jnp.maximum(t - 1, 0)

    def rows(order, tile):
        if order == "cls":
            return pl.BlockSpec((None, NCLS, tm // NCLS, d), lambda t: (tile(t) // tpb, 0, tile(t) % tpb, 0))
        return pl.BlockSpec((None, tm, d), lambda t: (tile(t) // tpb, tile(t) % tpb, 0))

    def mod_rows(tile):
        return pl.BlockSpec((None, 3 * N_SUB, d), lambda t: (tile(t) // tpb, 0, 0))

    out_shape = (b, sq, d) if order_out == "nat" else (b, NCLS, sq // NCLS, d)
    return pl.pallas_call(
        functools.partial(_ffn_body, s=s, order_in=order_in, order_out=order_out),
        grid=(ntiles + 1,),
        in_specs=[rows(order_in, cur), mod_rows(cur), mod_rows(prv),
                  _resident((1, d)), _resident((1, d)),
                  _resident(wg.shape), _resident(wu.shape), _resident(wo.shape)],
        out_specs=rows(order_out, prv),
        out_shape=jax.ShapeDtypeStruct(out_shape, F32),
        scratch_shapes=[pltpu.VMEM((tm, d), F32), pltpu.VMEM((tm, D_FF), F32), pltpu.VMEM((tm, D_FF), F32)],
        compiler_params=_cparams(("arbitrary",), 60),
        name=f"ffn_s{s}_{order_in}_{order_out}",
    )(x, mod, mod, lng.reshape(1, d), lnb.reshape(1, d), wg, wu, wo)


def _mproj_body(x_ref, mod_ref, w_ref, wgate_ref, gbias_ref, convw_ref,
                q_ref, kt_ref, v_ref, o_ref, bc_ref, cm_ref, gt_ref, pbuf):
    tm, d = x_ref.shape
    nh, dk = MLSTM_HEADS, MLSTM_DQK
    L = CELL_CHUNK
    j = pl.program_id(1)

    @pl.when(j == 0)
    def _():
        pbuf[0:8, :] = jnp.zeros((8, d), F32)

    @pl.when(j > 0)
    def _():
        pbuf[0:8, :] = pbuf[tm:tm + 8, :]

    h = _modulate(x_ref[...], mod_ref, 1).astype(BF16)
    p_qk = jnp.dot(h, w_ref[:, :d], preferred_element_type=F32)
    gg = jnp.dot(h, wgate_ref[...], preferred_element_type=F32) + gbias_ref[...]

    pbuf[8:tm + 8, :] = p_qk
    xe = pbuf[...]
    conv = convw_ref[0:1, :] * xe
    for i in range(1, CONV_WIDTH):
        conv = convw_ref[i:i + 1, :] * xe + pltpu.roll(conv, 1, 0)
    conv = conv[8:, :]
    qk = conv * jax.nn.sigmoid(conv)
    lane = lax.broadcasted_iota(jnp.int32, (tm, LANES), 1)
    for hd in range(nh):
        pair = qk[:, (hd // 2) * LANES:(hd // 2 + 1) * LANES] * (dk ** -0.5)
        keep = (lane >= (hd % 2) * dk) & (lane < (hd % 2 + 1) * dk)
        q_ref[:, hd * LANES:(hd + 1) * LANES] = jnp.where(keep, pair, 0.0).astype(BF16)
    kt_ref[...] = qk[:, nh * dk:].T.astype(BF16)

    v_ref[...] = jnp.dot(h, w_ref[:, d:2 * d], preferred_element_type=F32).astype(BF16)
    o_ref[...] = jnp.dot(h, w_ref[:, 2 * d:], preferred_element_type=F32).astype(BF16)

    ig = gg[:, :LANES]
    fp = gg[:, LANES:]
    logf = jnp.minimum(fp, 0.0) - jnp.log1p(jnp.exp(-jnp.abs(fp)))
    ri = lax.broadcasted_iota(jnp.int32, (L, L), 0)
    ci = lax.broadcasted_iota(jnp.int32, (L, L), 1)
    tril = jnp.where(ri >= ci, 1.0, 0.0).astype(F32)
    b = jnp.concatenate(
        [jnp.dot(tril, logf[c * L:(c + 1) * L, :], precision=lax.Precision.HIGHEST, preferred_element_type=F32)
         for c in range(tm // L)], axis=0)
    g = ig - b
    row_in_chunk = lax.broadcasted_iota(jnp.int32, (tm, LANES), 0) & (L - 1)
    cm = g
    shift = 1
    while shift < L:
        cm = jnp.maximum(cm, jnp.where(row_in_chunk >= shift, pltpu.roll(cm, shift, 0), -jnp.inf))
        shift *= 2
    bc_ref[...] = b
    cm_ref[...] = cm
    gt_ref[...] = g.T[:nh, :]


def _mlstm_proj(x, mod, w_in, gate_bias, conv_w, tm=512):
    b, sq, d = x.shape
    n_main = 3 * d
    n_gate = 2 * MLSTM_HEADS
    nk = MLSTM_HEADS * MLSTM_DQK
    w = w_in[:, :n_main].astype(BF16)
    nh = MLSTM_HEADS
    pad = ((0, 0), (0, LANES - nh))
    wgate = jnp.concatenate([jnp.pad(w_in[:, n_main:n_main + nh], pad),
                             jnp.pad(w_in[:, n_main + nh:], pad)], axis=1).astype(BF16)
    gb = gate_bias.astype(F32).reshape(1, n_gate)
    gbias = jnp.concatenate([jnp.pad(gb[:, :nh], pad), jnp.pad(gb[:, nh:], pad)], axis=1)
    row = pl.BlockSpec((None, tm, d), lambda i, j: (i, j, 0))
    lrow = pl.BlockSpec((None, tm, LANES), lambda i, j: (i, j, 0))
    act = jax.ShapeDtypeStruct((b, sq, d), BF16)
    gate_tile = jax.ShapeDtypeStruct((b, sq, LANES), F32)
    return pl.pallas_call(
        _mproj_body,
        grid=(b, sq // tm),
        in_specs=[row,
                  pl.BlockSpec((None, 3 * N_SUB, d), lambda i, j: (i, 0, 0)),
                  _resident(w.shape), _resident(wgate.shape), _resident(gbias.shape),
                  _resident((CONV_WIDTH, d))],
        out_specs=[row,
                   pl.BlockSpec((None, nk, tm), lambda i, j: (i, 0, j)),
                   row, row, lrow, lrow,
                   pl.BlockSpec((None, nh, tm), lambda i, j: (i, 0, j))],
        out_shape=[act, jax.ShapeDtypeStruct((b, nk, sq), BF16), act, act, gate_tile, gate_tile,
                   jax.ShapeDtypeStruct((b, nh, sq), F32)],
        scratch_shapes=[pltpu.VMEM((tm + 8, d), F32)],
        compiler_params=_cparams(("parallel", "arbitrary"), 48),
        name="mlstm_proj",
    )(x, mod, w, wgate, gbias, conv_w.astype(F32))


def _cell_body(q_ref, kt_ref, v_ref, o_ref, bc_ref, cm_ref, gt_ref, gain_ref, y_ref, cst, mst):
    @pl.when(pl.program_id(1) == 0)
    def _():
        cst[...] = jnp.zeros(cst.shape, F32)
        mst[...] = jnp.zeros(mst.shape, F32)

    for s in range(q_ref.shape[0] // CELL_CHUNK):
        _cell_chunk(s, q_ref, kt_ref, v_ref, o_ref, bc_ref, cm_ref, gt_ref, gain_ref, y_ref, cst, mst)


def _cell_chunk(s, q_ref, kt_ref, v_ref, o_ref, bc_ref, cm_ref, gt_ref, gain_ref, y_ref, cst, mst):
    L = CELL_CHUNK
    rows = slice(s * L, (s + 1) * L)
    nh, dk, dv = MLSTM_HEADS, MLSTM_DQK, MLSTM_DV
    heads = range(nh)

    m_prev = mst[...]
    bt = bc_ref[rows, :]
    cmt = cm_ref[rows, :]
    mt = jnp.maximum(m_prev, cmt)
    iwt = jnp.exp(m_prev - mt)
    emt = jnp.exp(-(bt + mt))
    b_last = bt[L - 1:L, :]
    cm_last = cmt[L - 1:L, :]
    m_loc = b_last + cm_last
    m_new = jnp.maximum(b_last + m_prev, m_loc)
    sp = jnp.exp(b_last + m_prev - m_new)
    sl = jnp.exp(m_loc - m_new)

    ri = lax.broadcasted_iota(jnp.int32, (L, L), 0)
    ci = lax.broadcasted_iota(jnp.int32, (L, L), 1)
    causal = ri >= ci
    g_row = [gt_ref[h:h + 1, rows] for h in heads]
    m_col = [mt[:, h:h + 1] for h in heads]
    dmat = [jnp.exp(jnp.where(causal, g_row[h] - m_col[h], -jnp.inf)) for h in heads]
    wa = [jnp.exp(g_row[h] - cm_last[:, h:h + 1]) for h in heads]

    ones = jnp.ones((L, LANES), BF16)
    q_h = [q_ref[rows, h * LANES:(h + 1) * LANES] for h in heads]
    kt_p = [kt_ref[p * LANES:(p + 1) * LANES, rows] for p in range(nh // 2)]
    v_aug = [jnp.concatenate([v_ref[rows, h * dv:(h + 1) * dv], ones], axis=1) for h in heads]
    c_pair = [jnp.concatenate([cst[2 * p], cst[2 * p + 1]], axis=0).astype(BF16) for p in range(nh // 2)]

    scores = [jnp.dot(q_h[h], kt_p[h // 2], preferred_element_type=F32) for h in heads]
    inter = [jnp.dot(q_h[h], c_pair[h // 2], preferred_element_type=F32) for h in heads]
    pm = [(scores[h] * dmat[h]).astype(BF16) for h in heads]
    iw = [iwt[:, h:h + 1] for h in heads]
    num_aug = [jnp.dot(pm[h], v_aug[h], preferred_element_type=F32) + iw[h] * inter[h] for h in heads]

    em = [emt[:, h:h + 1] for h in heads]
    hh = [num_aug[h][:, :dv] / jnp.maximum(jnp.abs(num_aug[h][:, dv:]), em[h]) for h in heads]
    ms = [jnp.mean(hh[h] * hh[h], axis=-1, keepdims=True) for h in heads]
    rs = [lax.rsqrt(ms[h] + RMS_EPS) for h in heads]
    hn = [hh[h] * rs[h] * gain_ref[:, h * dv:(h + 1) * dv] for h in heads]
    og = [o_ref[rows, h * dv:(h + 1) * dv].astype(F32) for h in heads]
    y_ref[rows, :] = jnp.concatenate([(jax.nn.sigmoid(og[h]) * hn[h]).astype(BF16) for h in heads], axis=1)

    kw = [(kt_p[h // 2][(h % 2) * dk:(h % 2 + 1) * dk, :].astype(F32) * wa[h]).astype(BF16) for h in heads]
    c_loc = [jnp.dot(kw[h], v_aug[h], preferred_element_type=F32) for h in heads]
    for h in heads:
        cst[h] = sp[:, h:h + 1] * cst[h] + sl[:, h:h + 1] * c_loc[h]
    mst[...] = m_new


def _mlstm_cell(q, kt, v, o, bc, cm, gt, head_gain):
    b, sq, d = v.shape
    L = CELL_CHUNKS_PER_STEP * CELL_CHUNK
    nk = kt.shape[1]
    row = pl.BlockSpec((None, L, d), lambda i, j: (i, j, 0))
    lrow = pl.BlockSpec((None, L, LANES), lambda i, j: (i, j, 0))
    return pl.pallas_call(
        _cell_body,
        grid=(b, sq // L),
        in_specs=[row,
                  pl.BlockSpec((None, nk, L), lambda i, j: (i, 0, j)),
                  row, row, lrow, lrow,
                  pl.BlockSpec((None, MLSTM_HEADS, L), lambda i, j: (i, 0, j)),
                  _resident((1, d))],
        out_specs=row,
        out_shape=jax.ShapeDtypeStruct((b, sq, d), BF16),
        scratch_shapes=[pltpu.VMEM((MLSTM_HEADS, MLSTM_DQK, 2 * MLSTM_DV), F32),
                        pltpu.VMEM((1, LANES), F32)],
        compiler_params=_cparams(("parallel", "arbitrary"), 32),
        name="mlstm_cell",
    )(q, kt, v, o, bc, cm, gt, head_gain.astype(F32).reshape(1, d))


def _mixer_out_body(y_ref, x_ref, mod_ref, lng_ref, lnb_ref, w_ref, o_ref):
    out = jnp.dot(y_ref[...], w_ref[...], preferred_element_type=F32)
    o_ref[...] = _post(x_ref[...], out, mod_ref, 1, 1.0, lng_ref, lnb_ref)


def _mixer_out(y, x, mod, lng, lnb, w_out, tm=512):
    b, sq, d = x.shape
    w = w_out.astype(BF16)
    row = pl.BlockSpec((None, tm, d), lambda i, j: (i, j, 0))
    return pl.pallas_call(
        _mixer_out_body,
        grid=(b, sq // tm),
        in_specs=[row, row,
                  pl.BlockSpec((None, 3 * N_SUB, d), lambda i, j: (i, 0, 0)),
                  _resident((1, d)), _resident((1, d)), _resident(w.shape)],
        out_specs=row,
        out_shape=jax.ShapeDtypeStruct(x.shape, F32),
        compiler_params=_cparams(("parallel", "parallel"), 32),
        name="mlstm_out",
    )(y, x, mod, lng.reshape(1, d), lnb.reshape(1, d), w)


def _aproj_body(x_ref, mod_ref, w_ref, o_ref):
    h = _modulate(x_ref[...], mod_ref, 1).astype(BF16)
    o_ref[...] = jnp.dot(h, w_ref[...], preferred_element_type=F32).astype(BF16)


def _attn_proj(x, mod, w_bf16, tm=512, tn=3072):
    b, sq, d = x.shape
    n = w_bf16.shape[1]
    return pl.pallas_call(
        _aproj_body,
        grid=(n // tn, b, sq // tm),
        in_specs=[pl.BlockSpec((None, tm, d), lambda g, i, j: (i, j, 0)),
                  pl.BlockSpec((None, 3 * N_SUB, d), lambda g, i, j: (i, 0, 0)),
                  pl.BlockSpec((d, tn), lambda g, i, j: (0, g))],
        out_specs=pl.BlockSpec((None, tm, tn), lambda g, i, j: (i, j, g)),
        out_shape=jax.ShapeDtypeStruct((b, sq, n), BF16),
        compiler_params=_cparams(("arbitrary", "arbitrary", "arbitrary"), 48),
        name="attn_proj",
    )(x, mod, w_bf16)


def _flat_rows(v):
    return v.reshape(-1, v.shape[-1])


def _attn_block(q, kc, vc, kp, vp, bias):
    has_prev = kp is not None
    dh = ATTN_HEAD_DIM
    heads = range(ATTN_HEADS)
    cols = [slice(h * dh, (h + 1) * dh) for h in heads]
    qb = q.shape[0]
    pb = kp.shape[0] if has_prev else 0
    lane = lax.broadcasted_iota(jnp.int32, (qb, LANES), 1)
    nt = (((1,), (1,)), ((), ()))

    scores = []
    for h in heads:
        s = lax.dot_general(q[:, cols[h]], kc[:, cols[h]], nt, preferred_element_type=F32)
        if has_prev:
            sp = lax.dot_general(q[:, cols[h]], kp[:, cols[h]], nt, preferred_element_type=F32)
            s = jnp.concatenate([sp, s], axis=1)
        scores.append(s)
    probs, maxes = [], []
    for h in heads:
        t = scores[h] * (dh ** -0.5 * LOG2E) + bias
        mx = jnp.max(t, axis=1, keepdims=True)
        probs.append(jnp.exp2(t - mx).astype(BF16))
        maxes.append(mx)
    accs = []
    for h in heads:
        acc = jnp.dot(probs[h][:, pb:], jnp.concatenate([vc[:, cols[h]], jnp.ones((qb, LANES), BF16)], axis=1),
                      preferred_element_type=F32)
        if has_prev:
            acc = acc + jnp.dot(probs[h][:, :pb],
                                jnp.concatenate([vp[:, cols[h]], jnp.ones((pb, LANES), BF16)], axis=1),
                                preferred_element_type=F32)
        accs.append(acc)
    outs = [(accs[h][:, :dh] / accs[h][:, dh:]).astype(BF16) for h in heads]
    lses = [maxes[h] * LN2 + jnp.log(accs[h][:, dh:]) for h in heads]
    lse_tile = lses[0]
    for h in heads[1:]:
        lse_tile = jnp.where(lane == h, lses[h], lse_tile)
    return jnp.concatenate(outs, axis=1), lse_tile


def _attn_whole_body(q_ref, k_ref, v_ref, bias_ref, o_ref, lse_ref):
    for c in range(q_ref.shape[0]):
        o, lse = _attn_block(q_ref[c], k_ref[c], v_ref[c], None, None, bias_ref[1])
        o_ref[c] = o
        lse_ref[c] = lse


def _attn_chain_body(q_ref, k_ref, v_ref, kp_ref, vp_ref, bias_ref, o_ref, lse_ref, *, n_axis):
    n = pl.program_id(n_axis)
    ur = q_ref.shape[1] // 2
    urp = kp_ref.shape[1]

    def part(ref, lo, hi):
        return _flat_rows(ref[:, lo:hi, :])

    o_a, l_a = _attn_block(part(q_ref, 0, ur), part(k_ref, 0, ur), part(v_ref, 0, ur),
                           _flat_rows(kp_ref[...]), _flat_rows(vp_ref[...]), bias_ref[jnp.minimum(n, 1)])
    o_b, l_b = _attn_block(part(q_ref, ur, 2 * ur), part(k_ref, ur, 2 * ur), part(v_ref, ur, 2 * ur),
                           part(k_ref, ur - urp, ur), part(v_ref, ur - urp, ur), bias_ref[1])
    nslab = q_ref.shape[0]
    o_ref[:, 0:ur, :] = o_a.reshape(nslab, ur, o_a.shape[-1])
    o_ref[:, ur:2 * ur, :] = o_b.reshape(nslab, ur, o_b.shape[-1])
    lse_ref[:, 0:ur, :] = l_a.reshape(nslab, ur, LANES)
    lse_ref[:, ur:2 * ur, :] = l_b.reshape(nslab, ur, LANES)


def _window_bias(nslab, cur_rows, prev_rows):
    qb = nslab * cur_rows
    pos_c = np.array([nslab * (i % cur_rows) + i // cur_rows for i in range(qb)])
    pb = nslab * prev_rows
    pos_p = np.array([nslab * (i % prev_rows) + i // prev_rows for i in range(pb)], dtype=np.int64) - pb
    kpos = np.concatenate([pos_p, pos_c])
    delta = pos_c[:, None] - kpos[None, :]
    ok = (delta >= 0) & (delta <= ATT_SPAN)
    later = np.where(ok, 0.0, -np.inf).astype(np.float32)
    first = later.copy()
    first[:, :pb] = -np.inf
    return jnp.asarray(np.stack([first, later]))


def _attn(proj, g, dil, d):
    b, ncls, u, _ = proj.shape
    qb = ATT_QB
    cq, ck, cv = 3 * g, 3 * g + 1, 3 * g + 2
    nsub = ATT_BLOCKS_PER_STEP
    if dil == NCLS:
        assert u == qb
        bias = _window_bias(1, qb, 0)
        arrs = (proj,) * 3
        def blk(col, width=d):
            return pl.BlockSpec((None, nsub, qb, width), lambda i, r: (i, r, 0, col))
        in_specs = [blk(cq), blk(ck), blk(cv)]
        o_spec, l_spec = blk(0), blk(0, LANES)
        grid = (b, ncls // nsub)
        o_shape, l_shape = (b, ncls, u, d), (b, ncls, u, LANES)
        sem = ("parallel", "parallel")
        body = _attn_whole_body
    else:
        nslab = ncls // dil
        ur = qb // nslab
        urp = max(ATT_SPAN // nslab, 16)
        bias = _window_bias(nslab, ur, urp)
        p5 = proj.reshape(b, nslab, dil, u, proj.shape[-1])
        arrs = (p5,) * 5
        step = nsub * ur
        def blk(col, width=d):
            return pl.BlockSpec((None, nslab, None, step, width), lambda i, r, n: (i, 0, r, n, col))
        def prev(col):
            return pl.BlockSpec((None, nslab, None, urp, d),
                                lambda i, r, n: (i, 0, r, jnp.maximum((step // urp) * n - 1, 0), col))
        in_specs = [blk(cq), blk(ck), blk(cv), prev(ck), prev(cv)]
        o_spec, l_spec = blk(0), blk(0, LANES)
        grid = (b, dil, u // step)
        o_shape, l_shape = (b, nslab, dil, u, d), (b, nslab, dil, u, LANES)
        sem = ("parallel", "parallel", "arbitrary")
        body = functools.partial(_attn_chain_body, n_axis=2)
    bias_spec = pl.BlockSpec(bias.shape, lambda *_: (0, 0, 0))
    o, lse = pl.pallas_call(
        body,
        grid=grid,
        in_specs=in_specs + [bias_spec],
        out_specs=[o_spec, l_spec],
        out_shape=[jax.ShapeDtypeStruct(o_shape, BF16), jax.ShapeDtypeStruct(l_shape, F32)],
        compiler_params=_cparams(sem, 40),
        name=f"attn_d{dil}",
    )(*arrs, bias)
    return o.reshape(b, ncls * u, d), lse.reshape(b, ncls * u, LANES)


def _amerge_body(o1_ref, o4_ref, o16_ref, l1_ref, l4_ref, l16_ref, x_ref, mod_ref, lng_ref, lnb_ref,
                 w_ref, out_ref):
    dh = ATTN_HEAD_DIM
    o_refs = (o1_ref, o4_ref, o16_ref)
    lses = [r[...] for r in (l1_ref, l4_ref, l16_ref)]
    heads = range(ATTN_HEADS)
    wts = []
    for h in heads:
        ls = [t[:, h:h + 1] for t in lses]
        mx = jnp.maximum(jnp.maximum(ls[0], ls[1]), ls[2])
        ws = [jnp.exp(t - mx) for t in ls]
        tot = ws[0] + ws[1] + ws[2]
        wts.append([w / tot for w in ws])
    ys = []
    for h in heads:
        cs = slice(h * dh, (h + 1) * dh)
        acc = wts[h][0] * o_refs[0][:, cs].astype(F32)
        for g in (1, 2):
            acc = acc + wts[h][g] * o_refs[g][:, cs].astype(F32)
        ys.append(acc.astype(BF16))
    out = jnp.dot(jnp.concatenate(ys, axis=1), w_ref[...], preferred_element_type=F32)
    out_ref[...] = _post(x_ref[...], out, mod_ref, 1, 1.0, lng_ref, lnb_ref)


def _attn_merge_out(o_list, lse_list, x, mod, lng, lnb, w_out, tm=512):
    b, sq, d = x.shape
    w = w_out.astype(BF16)
    row = pl.BlockSpec((None, tm, d), lambda i, j: (i, j, 0))
    lrow = pl.BlockSpec((None, tm, LANES), lambda i, j: (i, j, 0))
    return pl.pallas_call(
        _amerge_body,
        grid=(b, sq // tm),
        in_specs=[row, row, row, lrow, lrow, lrow, row,
                  pl.BlockSpec((None, 3 * N_SUB, d), lambda i, j: (i, 0, 0)),
                  _resident((1, d)), _resident((1, d)), _resident(w.shape)],
        out_specs=row,
        out_shape=jax.ShapeDtypeStruct(x.shape, F32),
        compiler_params=_cparams(("parallel", "parallel"), 32),
        name="attn_merge_out",
    )(*o_list, *lse_list, x, mod, lng.reshape(1, d), lnb.reshape(1, d), w)


def kernel(x, c, ada_w, ada_b, ln_g, ln_b, ffn_w_in, ffn_w_out, mlstm_w_in, mlstm_gate_bias, mlstm_conv_w,
           mlstm_head_gain, mlstm_w_out, attn_w_in, attn_w_out):
    b, sq, d = x.shape
    assert all(w // dl == ATT_SPAN for w, dl in DIL_GROUPS) and [dl for _, dl in DIL_GROUPS] == [1, 4, NCLS]
    assert sq == NCLS * ATT_QB and d == MLSTM_HEADS * MLSTM_DV == ATTN_HEADS * ATTN_HEAD_DIM
    mod_all = _ada(c, ada_w, ada_b)
    for layer in range(DEPTH):
        mod = mod_all[layer].reshape(b, 3 * N_SUB, d)
        j = layer // 2
        if layer % 2 == 0:
            x = _ffn(x, mod, ln_g[layer, 0], ln_b[layer, 0], ffn_w_in[layer, 0], ffn_w_out[layer, 0], 0)
            q, kt, v, o, bc, cm, gt = _mlstm_proj(x, mod, mlstm_w_in[j], mlstm_gate_bias[j], mlstm_conv_w[j])
            y = _mlstm_cell(q, kt, v, o, bc, cm, gt, mlstm_head_gain[j])
            x = _mixer_out(y, x, mod, ln_g[layer, 1], ln_b[layer, 1], mlstm_w_out[j])
            x = _ffn(x, mod, ln_g[layer, 2], ln_b[layer, 2], ffn_w_in[layer, 1], ffn_w_out[layer, 1], 2)
        else:
            xc = _ffn(x, mod, ln_g[layer, 0], ln_b[layer, 0], ffn_w_in[layer, 0], ffn_w_out[layer, 0], 0,
                      order_out="cls")
            xf = xc.reshape(b, sq, d)
            proj = _attn_proj(xf, mod, attn_w_in[j].astype(BF16)).reshape(b, NCLS, sq // NCLS, -1)
            outs, lses = [], []
            for g, (_, dil) in enumerate(DIL_GROUPS):
                o_g, lse_g = _attn(proj, g, dil, d)
                outs.append(o_g)
                lses.append(lse_g)
            xf = _attn_merge_out(outs, lses, xf, mod, ln_g[layer, 1], ln_b[layer, 1], attn_w_out[j])
            x = _ffn(xf.reshape(b, NCLS, sq // NCLS, d), mod, ln_g[layer, 2], ln_b[layer, 2],
                     ffn_w_in[layer, 1], ffn_w_out[layer, 1], 2, order_in="cls")
    return x
```

```python
import functools

import numpy as np
import jax
import jax.numpy as jnp
from jax import lax
from jax.experimental import pallas as pl
from jax.experimental.pallas import tpu as pltpu

F32 = jnp.float32
BF16 = jnp.bfloat16

DEPTH = 2
N_SUB = 3
D_FF = 2816
MLSTM_HEADS = 8
MLSTM_DQK = 64
MLSTM_DV = 128
CONV_WIDTH = 4
DIL_GROUPS = ((128, 1), (512, 4), (2048, 16))
ATTN_HEADS = 8
ATTN_HEAD_DIM = 128
ALPHA = (2 * DEPTH) ** 0.25
LN_EPS = 1e-5
RMS_EPS = 1e-6
LOG2E = 1.4426950408889634
LN2 = 0.6931471805599453

LANES = 128
NCLS = 16
CELL_CHUNK = 128
CELL_CHUNKS_PER_STEP = 2
CELL_HEADS_PER_STAGE = 8
ATT_SPAN = 128
ATT_QB = 256
ATT_BLOCKS_PER_STEP = 2
ATT_HEADS_PER_STAGE = 4
MIB = 1024 * 1024


def _cparams(sem, vmem_mib):
    return pltpu.CompilerParams(dimension_semantics=sem, vmem_limit_bytes=int(vmem_mib * MIB))


def _resident(shape):
    nd = len(shape)
    return pl.BlockSpec(shape, lambda *_: (0,) * nd, pipeline_mode=pl.Buffered(1))


def _layer_norm(y, g, b):
    mu = jnp.mean(y, axis=-1, keepdims=True)
    yc = y - mu
    var = jnp.mean(yc * yc, axis=-1, keepdims=True)
    return yc * lax.rsqrt(var + LN_EPS) * g + b


def _modulate(x, mod_ref, s):
    return x * (1.0 + mod_ref[3 * s + 1:3 * s + 2, :]) + mod_ref[3 * s:3 * s + 1, :]


def _post(x, out, mod_ref, s, weight, lng_ref, lnb_ref):
    y = ALPHA * x + (weight * (1.0 + mod_ref[3 * s + 2:3 * s + 3, :])) * out
    return _layer_norm(y, lng_ref[...], lnb_ref[...])


def _ada_body(c_ref, w_ref, b_ref, o_ref):
    c = c_ref[...]
    cond = c * jax.nn.sigmoid(c)
    o_ref[...] = jnp.dot(cond, w_ref[...], precision=lax.Precision.HIGHEST,
                         preferred_element_type=F32) + b_ref[...]


def _ada(c, ada_w, ada_b):
    depth, d, n = ada_w.shape
    b = c.shape[0]
    tn = 1152
    return pl.pallas_call(
        _ada_body,
        grid=(depth, n // tn),
        in_specs=[pl.BlockSpec((b, d), lambda l, j: (0, 0)),
                  pl.BlockSpec((None, d, tn), lambda l, j: (l, 0, j)),
                  pl.BlockSpec((None, 1, tn), lambda l, j: (l, 0, j))],
        out_specs=pl.BlockSpec((None, b, tn), lambda l, j: (l, 0, j)),
        out_shape=jax.ShapeDtypeStruct((depth, b, n), F32),
        compiler_params=_cparams(("arbitrary", "arbitrary"), 32),
        name="ada_mod",
    )(c, ada_w, ada_b.reshape(depth, 1, n))


def _ffn_body(x_ref, mod_ref, lng_ref, lnb_ref, wg_ref, wu_ref, wo_ref, o_ref, *, s, order_in, order_out):
    if order_in == "cls":
        tm = x_ref.shape[0] * x_ref.shape[1]
        x = x_ref[...].reshape(tm, x_ref.shape[2])
    elif order_out == "cls":
        tm, d = x_ref.shape
        x = jnp.swapaxes(x_ref[...].reshape(tm // NCLS, NCLS, d), 0, 1).reshape(tm, d)
    else:
        tm = x_ref.shape[0]
        x = x_ref[...]
    h = _modulate(x, mod_ref, s).astype(BF16)
    g = jnp.dot(h, wg_ref[...], preferred_element_type=F32)
    u = jnp.dot(h, wu_ref[...], preferred_element_type=F32)
    a = (g * jax.nn.sigmoid(g) * u).astype(BF16)
    out = jnp.dot(a, wo_ref[...], preferred_element_type=F32)
    y = _post(x, out, mod_ref, s, 0.5, lng_ref, lnb_ref)
    if order_out == "cls":
        o_ref[...] = y.reshape(o_ref.shape)
    elif order_in == "cls":
        o_ref[...] = jnp.swapaxes(y.reshape(NCLS, tm // NCLS, y.shape[1]), 0, 1).reshape(o_ref.shape)
    else:
        o_ref[...] = y


def _ffn(x, mod, lng, lnb, w_in, w_out, s, order_in="nat", order_out="nat", tm=512):
    d = x.shape[-1]
    b = x.shape[0]
    sq = x.shape[1] if order_in == "nat" else x.shape[1] * x.shape[2]
    wg = w_in[:, :D_FF].astype(BF16)
    wu = w_in[:, D_FF:].astype(BF16)
    wo = w_out.astype(BF16)
    nat = pl.BlockSpec((None, tm, d), lambda i, j: (i, j, 0))
    cls = pl.BlockSpec((None, NCLS, tm // NCLS, d), lambda i, j: (i, 0, j, 0))
    out_shape = (b, sq, d) if order_out == "nat" else (b, NCLS, sq // NCLS, d)
    return pl.pallas_call(
        functools.partial(_ffn_body, s=s, order_in=order_in, order_out=order_out),
        grid=(b, sq // tm),
        in_specs=[cls if order_in == "cls" else nat,
                  pl.BlockSpec((None, 3 * N_SUB, d), lambda i, j: (i, 0, 0)),
                  _resident((1, d)), _resident((1, d)),
                  _resident(wg.shape), _resident(wu.shape), _resident(wo.shape)],
        out_specs=cls if order_out == "cls" else nat,
        out_shape=jax.ShapeDtypeStruct(out_shape, F32),
        compiler_params=_cparams(("parallel", "parallel"), 56),
        name=f"ffn_s{s}_{order_in}_{order_out}",
    )(x, mod, lng.reshape(1, d), lnb.reshape(1, d), wg, wu, wo)


def _mproj_body(x_ref, mod_ref, w_ref, wgate_ref, gbias_ref, convw_ref,
                q_ref, kt_ref, v_ref, o_ref, bc_ref, cm_ref, gt_ref, pbuf):
    tm, d = x_ref.shape
    nh, dk = MLSTM_HEADS, MLSTM_DQK
    L = CELL_CHUNK
    j = pl.program_id(1)

    @pl.when(j == 0)
    def _():
        pbuf[0:8, :] = jnp.zeros((8, d), F32)

    @pl.when(j > 0)
    def _():
        pbuf[0:8, :] = pbuf[tm:tm + 8, :]

    h = _modulate(x_ref[...], mod_ref, 1).astype(BF16)
    p_qk = jnp.dot(h, w_ref[:, :d], preferred_element_type=F32)
    gg = jnp.dot(h, wgate_ref[...], preferred_element_type=F32) + gbias_ref[...]
    v_ref[...] = jnp.dot(h, w_ref[:, d:2 * d], preferred_element_type=F32).astype(BF16)

    pbuf[8:tm + 8, :] = p_qk
    last = CONV_WIDTH - 1
    conv = convw_ref[last:last + 1, :] * pbuf[8:8 + tm, :]
    for i in range(last - 1, -1, -1):
        conv = conv + convw_ref[i:i + 1, :] * pbuf[8 - last + i:8 - last + i + tm, :]
    qk = conv * jax.nn.sigmoid(conv)
    lane = lax.broadcasted_iota(jnp.int32, (tm, LANES), 1)
    for hd in range(nh):
        pair = qk[:, (hd // 2) * LANES:(hd // 2 + 1) * LANES] * (dk ** -0.5)
        keep = (lane >= (hd % 2) * dk) & (lane < (hd % 2 + 1) * dk)
        q_ref[:, hd * LANES:(hd + 1) * LANES] = jnp.where(keep, pair, 0.0).astype(BF16)
    kt_ref[...] = qk[:, nh * dk:].T.astype(BF16)

    o_ref[...] = jnp.dot(h, w_ref[:, 2 * d:], preferred_element_type=F32).astype(BF16)

    ig = gg[:, :LANES]
    fp = gg[:, LANES:]
    logf = jnp.minimum(fp, 0.0) - jnp.log1p(jnp.exp(-jnp.abs(fp)))
    ri = lax.broadcasted_iota(jnp.int32, (L, L), 0)
    ci = lax.broadcasted_iota(jnp.int32, (L, L), 1)
    tril = jnp.where(ri >= ci, 1.0, 0.0).astype(F32)
    b = jnp.concatenate(
        [jnp.dot(tril, logf[c * L:(c + 1) * L, :], precision=lax.Precision.HIGHEST, preferred_element_type=F32)
         for c in range(tm // L)], axis=0)
    g = ig - b
    row_in_chunk = lax.broadcasted_iota(jnp.int32, (tm, LANES), 0) & (L - 1)
    cm = g
    shift = 1
    while shift < L:
        cm = jnp.maximum(cm, jnp.where(row_in_chunk >= shift, pltpu.roll(cm, shift, 0), -jnp.inf))
        shift *= 2
    bc_ref[...] = b
    cm_ref[...] = cm
    gt_ref[...] = g.T[:nh, :]


def _mlstm_proj(x, mod, w_in, gate_bias, conv_w, tm=512):
    b, sq, d = x.shape
    n_main = 3 * d
    n_gate = 2 * MLSTM_HEADS
    nk = MLSTM_HEADS * MLSTM_DQK
    w = w_in[:, :n_main].astype(BF16)
    nh = MLSTM_HEADS
    pad = ((0, 0), (0, LANES - nh))
    wgate = jnp.concatenate([jnp.pad(w_in[:, n_main:n_main + nh], pad),
                             jnp.pad(w_in[:, n_main + nh:], pad)], axis=1).astype(BF16)
    gb = gate_bias.astype(F32).reshape(1, n_gate)
    gbias = jnp.concatenate([jnp.pad(gb[:, :nh], pad), jnp.pad(gb[:, nh:], pad)], axis=1)
    row = pl.BlockSpec((None, tm, d), lambda i, j: (i, j, 0))
    lrow = pl.BlockSpec((None, tm, LANES), lambda i, j: (i, j, 0))
    act = jax.ShapeDtypeStruct((b, sq, d), BF16)
    gate_tile = jax.ShapeDtypeStruct((b, sq, LANES), F32)
    return pl.pallas_call(
        _mproj_body,
        grid=(b, sq // tm),
        in_specs=[row,
                  pl.BlockSpec((None, 3 * N_SUB, d), lambda i, j: (i, 0, 0)),
                  _resident(w.shape), _resident(wgate.shape), _resident(gbias.shape),
                  _resident((CONV_WIDTH, d))],
        out_specs=[row,
                   pl.BlockSpec((None, nk, tm), lambda i, j: (i, 0, j)),
                   row, row, lrow, lrow,
                   pl.BlockSpec((None, nh, tm), lambda i, j: (i, 0, j))],
        out_shape=[act, jax.ShapeDtypeStruct((b, nk, sq), BF16), act, act, gate_tile, gate_tile,
                   jax.ShapeDtypeStruct((b, nh, sq), F32)],
        scratch_shapes=[pltpu.VMEM((tm + 8, d), F32)],
        compiler_params=_cparams(("parallel", "arbitrary"), 48),
        name="mlstm_proj",
    )(x, mod, w, wgate, gbias, conv_w.astype(F32))


def _cell_body(q_ref, kt_ref, v_ref, o_ref, bc_ref, cm_ref, gt_ref, gain_ref, y_ref, cst, mst):
    @pl.when(pl.program_id(1) == 0)
    def _():
        cst[...] = jnp.zeros(cst.shape, F32)
        mst[...] = jnp.zeros(mst.shape, F32)

    for s in range(q_ref.shape[0] // CELL_CHUNK):
        _cell_chunk(s, q_ref, kt_ref, v_ref, o_ref, bc_ref, cm_ref, gt_ref, gain_ref, y_ref, cst, mst)


def _cell_chunk(s, q_ref, kt_ref, v_ref, o_ref, bc_ref, cm_ref, gt_ref, gain_ref, y_ref, cst, mst):
    L = CELL_CHUNK
    rows = slice(s * L, (s + 1) * L)
    nh, dk, dv = MLSTM_HEADS, MLSTM_DQK, MLSTM_DV
    heads = range(nh)

    m_prev = mst[...]
    bt = bc_ref[rows, :]
    cmt = cm_ref[rows, :]
    mt = jnp.maximum(m_prev, cmt)
    iwt = jnp.exp(m_prev - mt)
    emt = jnp.exp(-(bt + mt))
    b_last = bt[L - 1:L, :]
    cm_last = cmt[L - 1:L, :]
    m_loc = b_last + cm_last
    m_new = jnp.maximum(b_last + m_prev, m_loc)
    sp = jnp.exp(b_last + m_prev - m_new)
    sl = jnp.exp(m_loc - m_new)

    ri = lax.broadcasted_iota(jnp.int32, (L, L), 0)
    ci = lax.broadcasted_iota(jnp.int32, (L, L), 1)
    causal = ri >= ci
    ones = jnp.ones((L, LANES), BF16)
    kt_p = [kt_ref[p * LANES:(p + 1) * LANES, rows] for p in range(nh // 2)]
    outs = {}
    for h0 in range(0, nh, CELL_HEADS_PER_STAGE):
        heads = range(h0, h0 + CELL_HEADS_PER_STAGE)
        g_row = {h: gt_ref[h:h + 1, rows] for h in heads}
        m_col = {h: mt[:, h:h + 1] for h in heads}
        dmat = {h: jnp.exp(jnp.where(causal, g_row[h] - m_col[h], -jnp.inf)) for h in heads}
        wa = {h: jnp.exp(g_row[h] - cm_last[:, h:h + 1]) for h in heads}

        q_h = {h: q_ref[rows, h * LANES:(h + 1) * LANES] for h in heads}
        v_aug = {h: jnp.concatenate([v_ref[rows, h * dv:(h + 1) * dv], ones], axis=1) for h in heads}
        c_pair = {p: jnp.concatenate([cst[2 * p], cst[2 * p + 1]], axis=0).astype(BF16)
                  for p in range(h0 // 2, (h0 + CELL_HEADS_PER_STAGE) // 2)}

        scores = {h: jnp.dot(q_h[h], kt_p[h // 2], preferred_element_type=F32) for h in heads}
        inter = {h: jnp.dot(q_h[h], c_pair[h // 2], preferred_element_type=F32) for h in heads}
        pm = {h: (scores[h] * dmat[h]).astype(BF16) for h in heads}
        iw = {h: iwt[:, h:h + 1] for h in heads}
        num_aug = {h: jnp.dot(pm[h], v_aug[h], preferred_element_type=F32) + iw[h] * inter[h] for h in heads}

        em = {h: emt[:, h:h + 1] for h in heads}
        hh = {h: num_aug[h][:, :dv] / jnp.maximum(jnp.abs(num_aug[h][:, dv:]), em[h]) for h in heads}
        ms = {h: jnp.mean(hh[h] * hh[h], axis=-1, keepdims=True) for h in heads}
        rs = {h: lax.rsqrt(ms[h] + RMS_EPS) for h in heads}
        hn = {h: hh[h] * rs[h] * gain_ref[:, h * dv:(h + 1) * dv] for h in heads}
        og = {h: o_ref[rows, h * dv:(h + 1) * dv].astype(F32) for h in heads}
        outs.update({h: (jax.nn.sigmoid(og[h]) * hn[h]).astype(BF16) for h in heads})

        kw = {h: (kt_p[h // 2][(h % 2) * dk:(h % 2 + 1) * dk, :].astype(F32) * wa[h]).astype(BF16) for h in heads}
        c_loc = {h: jnp.dot(kw[h], v_aug[h], preferred_element_type=F32) for h in heads}
        for h in heads:
            cst[h] = sp[:, h:h + 1] * cst[h] + sl[:, h:h + 1] * c_loc[h]
    y_ref[rows, :] = jnp.concatenate([outs[h] for h in range(nh)], axis=1)
    mst[...] = m_new


def _mlstm_cell(q, kt, v, o, bc, cm, gt, head_gain):
    b, sq, d = v.shape
    L = CELL_CHUNKS_PER_STEP * CELL_CHUNK
    nk = kt.shape[1]
    row = pl.BlockSpec((None, L, d), lambda i, j: (i, j, 0))
    lrow = pl.BlockSpec((None, L, LANES), lambda i, j: (i, j, 0))
    return pl.pallas_call(
        _cell_body,
        grid=(b, sq // L),
        in_specs=[row,
                  pl.BlockSpec((None, nk, L), lambda i, j: (i, 0, j)),
                  row, row, lrow, lrow,
                  pl.BlockSpec((None, MLSTM_HEADS, L), lambda i, j: (i, 0, j)),
                  _resident((1, d))],
        out_specs=row,
        out_shape=jax.ShapeDtypeStruct((b, sq, d), BF16),
        scratch_shapes=[pltpu.VMEM((MLSTM_HEADS, MLSTM_DQK, 2 * MLSTM_DV), F32),
                        pltpu.VMEM((1, LANES), F32)],
        compiler_params=_cparams(("parallel", "arbitrary"), 32),
        name="mlstm_cell",
    )(q, kt, v, o, bc, cm, gt, head_gain.astype(F32).reshape(1, d))


def _mixer_out_body(y_ref, x_ref, mod_ref, lng_ref, lnb_ref, w_ref, o_ref):
    out = jnp.dot(y_ref[...], w_ref[...], preferred_element_type=F32)
    o_ref[...] = _post(x_ref[...], out, mod_ref, 1, 1.0, lng_ref, lnb_ref)


def _mixer_out(y, x, mod, lng, lnb, w_out, tm=512):
    b, sq, d = x.shape
    w = w_out.astype(BF16)
    row = pl.BlockSpec((None, tm, d), lambda i, j: (i, j, 0))
    return pl.pallas_call(
        _mixer_out_body,
        grid=(b, sq // tm),
        in_specs=[row, row,
                  pl.BlockSpec((None, 3 * N_SUB, d), lambda i, j: (i, 0, 0)),
                  _resident((1, d)), _resident((1, d)), _resident(w.shape)],
        out_specs=row,
        out_shape=jax.ShapeDtypeStruct(x.shape, F32),
        compiler_params=_cparams(("parallel", "parallel"), 32),
        name="mlstm_out",
    )(y, x, mod, lng.reshape(1, d), lnb.reshape(1, d), w)


def _aproj_body(x_ref, mod_ref, w_ref, o_ref):
    h = _modulate(x_ref[...], mod_ref, 1).astype(BF16)
    o_ref[...] = jnp.dot(h, w_ref[...], preferred_element_type=F32).astype(BF16)


def _attn_proj(x, mod, w_bf16, tm=512, tn=3072):
    b, sq, d = x.shape
    n = w_bf16.shape[1]
    return pl.pallas_call(
        _aproj_body,
        grid=(n // tn, b, sq // tm),
        in_specs=[pl.BlockSpec((None, tm, d), lambda g, i, j: (i, j, 0)),
                  pl.BlockSpec((None, 3 * N_SUB, d), lambda g, i, j: (i, 0, 0)),
                  pl.BlockSpec((d, tn), lambda g, i, j: (0, g))],
        out_specs=pl.BlockSpec((None, tm, tn), lambda g, i, j: (i, j, g)),
        out_shape=jax.ShapeDtypeStruct((b, sq, n), BF16),
        compiler_params=_cparams(("arbitrary", "arbitrary", "arbitrary"), 48),
        name="attn_proj",
    )(x, mod, w_bf16)


def _flat_rows(v):
    return v.reshape(-1, v.shape[-1])


def _attn_block(q, kc, vc, kp, vp, bias):
    has_prev = kp is not None
    dh = ATTN_HEAD_DIM
    cols = [slice(h * dh, (h + 1) * dh) for h in range(ATTN_HEADS)]
    qb = q.shape[0]
    pb = kp.shape[0] if has_prev else 0
    lane = lax.broadcasted_iota(jnp.int32, (qb, LANES), 1)
    nt = (((1,), (1,)), ((), ()))

    outs, lses = [], []
    for h0 in range(0, ATTN_HEADS, ATT_HEADS_PER_STAGE):
        heads = range(h0, h0 + ATT_HEADS_PER_STAGE)
        scores = {}
        for h in heads:
            s = lax.dot_general(q[:, cols[h]], kc[:, cols[h]], nt, preferred_element_type=F32)
            if has_prev:
                sp = lax.dot_general(q[:, cols[h]], kp[:, cols[h]], nt, preferred_element_type=F32)
                s = jnp.concatenate([sp, s], axis=1)
            scores[h] = s
        probs, maxes = {}, {}
        for h in heads:
            t = scores[h] * (dh ** -0.5 * LOG2E) + bias
            maxes[h] = jnp.max(t, axis=1, keepdims=True)
            probs[h] = jnp.exp2(t - maxes[h]).astype(BF16)
        accs = {}
        for h in heads:
            acc = jnp.dot(probs[h][:, pb:], jnp.concatenate([vc[:, cols[h]], jnp.ones((qb, LANES), BF16)], axis=1),
                          preferred_element_type=F32)
            if has_prev:
                acc = acc + jnp.dot(probs[h][:, :pb],
                                    jnp.concatenate([vp[:, cols[h]], jnp.ones((pb, LANES), BF16)], axis=1),
                                    preferred_element_type=F32)
            accs[h] = acc
        outs += [(accs[h][:, :dh] / accs[h][:, dh:]).astype(BF16) for h in heads]
        lses += [maxes[h] * LN2 + jnp.log(accs[h][:, dh:]) for h in heads]
    lse_tile = lses[0]
    for h in range(1, ATTN_HEADS):
        lse_tile = jnp.where(lane == h, lses[h], lse_tile)
    return jnp.concatenate(outs, axis=1), lse_tile


def _attn_whole_body(q_ref, k_ref, v_ref, bias_ref, o_ref, lse_ref):
    for c in range(q_ref.shape[0]):
        o, lse = _attn_block(q_ref[c], k_ref[c], v_ref[c], None, None, bias_ref[1])
        o_ref[c] = o
        lse_ref[c] = lse


def _attn_chain_body(q_ref, k_ref, v_ref, kp_ref, vp_ref, bias_ref, o_ref, lse_ref, *, n_axis):
    n = pl.program_id(n_axis)
    ur = q_ref.shape[1] // 2
    urp = kp_ref.shape[1]

    def part(ref, lo, hi):
        return _flat_rows(ref[:, lo:hi, :])

    o_a, l_a = _attn_block(part(q_ref, 0, ur), part(k_ref, 0, ur), part(v_ref, 0, ur),
                           _flat_rows(kp_ref[...]), _flat_rows(vp_ref[...]), bias_ref[jnp.minimum(n, 1)])
    o_b, l_b = _attn_block(part(q_ref, ur, 2 * ur), part(k_ref, ur, 2 * ur), part(v_ref, ur, 2 * ur),
                           part(k_ref, ur - urp, ur), part(v_ref, ur - urp, ur), bias_ref[1])
    nslab = q_ref.shape[0]
    o_ref[:, 0:ur, :] = o_a.reshape(nslab, ur, o_a.shape[-1])
    o_ref[:, ur:2 * ur, :] = o_b.reshape(nslab, ur, o_b.shape[-1])
    lse_ref[:, 0:ur, :] = l_a.reshape(nslab, ur, LANES)
    lse_ref[:, ur:2 * ur, :] = l_b.reshape(nslab, ur, LANES)


def _window_bias(nslab, cur_rows, prev_rows):
    qb = nslab * cur_rows
    pos_c = np.array([nslab * (i % cur_rows) + i // cur_rows for i in range(qb)])
    pb = nslab * prev_rows
    pos_p = np.array([nslab * (i % prev_rows) + i // prev_rows for i in range(pb)], dtype=np.int64) - pb
    kpos = np.concatenate([pos_p, pos_c])
    delta = pos_c[:, None] - kpos[None, :]
    ok = (delta >= 0) & (delta <= ATT_SPAN)
    later = np.where(ok, 0.0, -np.inf).astype(np.float32)
    first = later.copy()
    first[:, :pb] = -np.inf
    return jnp.asarray(np.stack([first, later]))


def _attn(proj, g, dil, d):
    b, ncls, u, _ = proj.shape
    qb = ATT_QB
    cq, ck, cv = 3 * g, 3 * g + 1, 3 * g + 2
    nsub = ATT_BLOCKS_PER_STEP
    if dil == NCLS:
        assert u == qb
        bias = _window_bias(1, qb, 0)
        arrs = (proj,) * 3
        def blk(col, width=d):
            return pl.BlockSpec((None, nsub, qb, width), lambda i, r: (i, r, 0, col))
        in_specs = [blk(cq), blk(ck), blk(cv)]
        o_spec, l_spec = blk(0), blk(0, LANES)
        grid = (b, ncls // nsub)
        o_shape, l_shape = (b, ncls, u, d), (b, ncls, u, LANES)
        sem = ("parallel", "parallel")
        body = _attn_whole_body
    else:
        nslab = ncls // dil
        ur = qb // nslab
        urp = max(ATT_SPAN // nslab, 16)
        bias = _window_bias(nslab, ur, urp)
        p5 = proj.reshape(b, nslab, dil, u, proj.shape[-1])
        arrs = (p5,) * 5
        step = nsub * ur
        def blk(col, width=d):
            return pl.BlockSpec((None, nslab, None, step, width), lambda i, r, n: (i, 0, r, n, col))
        def prev(col):
            return pl.BlockSpec((None, nslab, None, urp, d),
                                lambda i, r, n: (i, 0, r, jnp.maximum((step // urp) * n - 1, 0), col))
        in_specs = [blk(cq), blk(ck), blk(cv), prev(ck), prev(cv)]
        o_spec, l_spec = blk(0), blk(0, LANES)
        grid = (b, dil, u // step)
        o_shape, l_shape = (b, nslab, dil, u, d), (b, nslab, dil, u, LANES)
        sem = ("parallel", "parallel", "arbitrary")
        body = functools.partial(_attn_chain_body, n_axis=2)
    bias_spec = pl.BlockSpec(bias.shape, lambda *_: (0, 0, 0))
    o, lse = pl.pallas_call(
        body,
        grid=grid,
        in_specs=in_specs + [bias_spec],
        out_specs=[o_spec, l_spec],
        out_shape=[jax.ShapeDtypeStruct(o_shape, BF16), jax.ShapeDtypeStruct(l_shape, F32)],
        compiler_params=_cparams(sem, 40),
        name=f"attn_d{dil}",
    )(*arrs, bias)
    return o.reshape(b, ncls * u, d), lse.reshape(b, ncls * u, LANES)


def _amerge_body(o1_ref, o4_ref, o16_ref, l1_ref, l4_ref, l16_ref, x_ref, mod_ref, lng_ref, lnb_ref,
                 w_ref, out_ref):
    dh = ATTN_HEAD_DIM
    heads = range(ATTN_HEADS)
    l1, l4, l16 = l1_ref[...], l4_ref[...], l16_ref[...]
    mx = jnp.maximum(jnp.maximum(l1, l4), l16)
    e1, e4, e16 = jnp.exp(l1 - mx), jnp.exp(l4 - mx), jnp.exp(l16 - mx)
    tot = e1 + e4 + e16
    w1 = e1 / tot
    w4 = e4 / tot
    w1_h = [w1[:, h:h + 1] for h in heads]
    w4_h = [w4[:, h:h + 1] for h in heads]
    ys = []
    for h in heads:
        cs = slice(h * dh, (h + 1) * dh)
        o16 = o16_ref[:, cs].astype(F32)
        acc = o16 + w1_h[h] * (o1_ref[:, cs].astype(F32) - o16) + w4_h[h] * (o4_ref[:, cs].astype(F32) - o16)
        ys.append(acc.astype(BF16))
    out = jnp.dot(jnp.concatenate(ys, axis=1), w_ref[...], preferred_element_type=F32)
    out_ref[...] = _post(x_ref[...], out, mod_ref, 1, 1.0, lng_ref, lnb_ref)


def _attn_merge_out(o_list, lse_list, x, mod, lng, lnb, w_out, tm=512):
    b, sq, d = x.shape
    w = w_out.astype(BF16)
    row = pl.BlockSpec((None, tm, d), lambda i, j: (i, j, 0))
    lrow = pl.BlockSpec((None, tm, LANES), lambda i, j: (i, j, 0))
    return pl.pallas_call(
        _amerge_body,
        grid=(b, sq // tm),
        in_specs=[row, row, row, lrow, lrow, lrow, row,
                  pl.BlockSpec((None, 3 * N_SUB, d), lambda i, j: (i, 0, 0)),
                  _resident((1, d)), _resident((1, d)), _resident(w.shape)],
        out_specs=row,
        out_shape=jax.ShapeDtypeStruct(x.shape, F32),
        compiler_params=_cparams(("parallel", "parallel"), 32),
        name="attn_merge_out",
    )(*o_list, *lse_list, x, mod, lng.reshape(1, d), lnb.reshape(1, d), w)


def kernel(x, c, ada_w, ada_b, ln_g, ln_b, ffn_w_in, ffn_w_out, mlstm_w_in, mlstm_gate_bias, mlstm_conv_w,
           mlstm_head_gain, mlstm_w_out, attn_w_in, attn_w_out):
    b, sq, d = x.shape
    assert all(w // dl == ATT_SPAN for w, dl in DIL_GROUPS) and [dl for _, dl in DIL_GROUPS] == [1, 4, NCLS]
    assert sq == NCLS * ATT_QB and d == MLSTM_HEADS * MLSTM_DV == ATTN_HEADS * ATTN_HEAD_DIM
    mod_all = _ada(c, ada_w, ada_b)
    for layer in range(DEPTH):
        mod = mod_all[layer].reshape(b, 3 * N_SUB, d)
        j = layer // 2
        if layer % 2 == 0:
            x = _ffn(x, mod, ln_g[layer, 0], ln_b[layer, 0], ffn_w_in[layer, 0], ffn_w_out[layer, 0], 0)
            q, kt, v, o, bc, cm, gt = _mlstm_proj(x, mod, mlstm_w_in[j], mlstm_gate_bias[j], mlstm_conv_w[j])
            y = _mlstm_cell(q, kt, v, o, bc, cm, gt, mlstm_head_gain[j])
            x = _mixer_out(y, x, mod, ln_g[layer, 1], ln_b[layer, 1], mlstm_w_out[j])
            x = _ffn(x, mod, ln_g[layer, 2], ln_b[layer, 2], ffn_w_in[layer, 1], ffn_w_out[layer, 1], 2)
        else:
            xc = _ffn(x, mod, ln_g[layer, 0], ln_b[layer, 0], ffn_w_in[layer, 0], ffn_w_out[layer, 0], 0,
                      order_out="cls")
            xf = xc.reshape(b, sq, d)
            proj = _attn_proj(xf, mod, attn_w_in[j].astype(BF16)).reshape(b, NCLS, sq // NCLS, -1)
            outs, lses = [], []
            for g, (_, dil) in enumerate(DIL_GROUPS):
                o_g, lse_g = _attn(proj, g, dil, d)
                outs.append(o_g)
                lses.append(lse_g)
            xf = _attn_merge_out(outs, lses, xf, mod, ln_g[layer, 1], ln_b[layer, 1], attn_w_out[j])
            x = _ffn(xf.reshape(b, NCLS, sq // NCLS, d), mod, ln_g[layer, 2], ln_b[layer, 2],
                     ffn_w_in[layer, 1], ffn_w_out[layer, 1], 2, order_in="cls")
    return x
```

```python
import functools

import numpy as np
import jax
import jax.numpy as jnp
from jax import lax
from jax.experimental import pallas as pl
from jax.experimental.pallas import tpu as pltpu

F32 = jnp.float32
BF16 = jnp.bfloat16

DEPTH = 2
N_SUB = 3
D_FF = 2816
MLSTM_HEADS = 8
MLSTM_DQK = 64
MLSTM_DV = 128
CONV_WIDTH = 4
DIL_GROUPS = ((128, 1), (512, 4), (2048, 16))
ATTN_HEADS = 8
ATTN_HEAD_DIM = 128
ALPHA = (2 * DEPTH) ** 0.25
LN_EPS = 1e-5
RMS_EPS = 1e-6
LOG2E = 1.4426950408889634
LN2 = 0.6931471805599453

LANES = 128
NCLS = 16
CELL_CHUNK = 128
CELL_CHUNKS_PER_STEP = 4
CELL_HEADS_PER_STAGE = 8
ATT_SPAN = 128
ATT_QB = 256
ATT_BLOCKS_PER_STEP = 2
ATT_SEQS_PER_STEP = 4
ATT_HEADS_PER_STAGE_NO_PREV = 4
ATT_HEADS_PER_STAGE_PREV = 8
MIB = 1024 * 1024


def _cparams(sem, vmem_mib):
    return pltpu.CompilerParams(dimension_semantics=sem, vmem_limit_bytes=int(vmem_mib * MIB))


def _resident(shape):
    nd = len(shape)
    return pl.BlockSpec(shape, lambda *_: (0,) * nd, pipeline_mode=pl.Buffered(1))


def _layer_norm(y, g, b):
    mu = jnp.mean(y, axis=-1, keepdims=True)
    yc = y - mu
    var = jnp.mean(yc * yc, axis=-1, keepdims=True)
    return yc * lax.rsqrt(var + LN_EPS) * g + b


def _modulate(x, mod_ref, s):
    return x * (1.0 + mod_ref[3 * s + 1:3 * s + 2, :]) + mod_ref[3 * s:3 * s + 1, :]


def _post(x, out, mod_ref, s, weight, lng_ref, lnb_ref):
    y = ALPHA * x + (weight * (1.0 + mod_ref[3 * s + 2:3 * s + 3, :])) * out
    return _layer_norm(y, lng_ref[...], lnb_ref[...])


def _ada_body(c_ref, w_ref, b_ref, o_ref):
    c = c_ref[...]
    cond = c * jax.nn.sigmoid(c)
    o_ref[...] = jnp.dot(cond, w_ref[...], precision=lax.Precision.HIGHEST,
                         preferred_element_type=F32) + b_ref[...]


def _ada(c, ada_w, ada_b):
    depth, d, n = ada_w.shape
    b = c.shape[0]
    tn = 1152
    return pl.pallas_call(
        _ada_body,
        grid=(depth, n // tn),
        in_specs=[pl.BlockSpec((b, d), lambda l, j: (0, 0)),
                  pl.BlockSpec((None, d, tn), lambda l, j: (l, 0, j)),
                  pl.BlockSpec((None, 1, tn), lambda l, j: (l, 0, j))],
        out_specs=pl.BlockSpec((None, b, tn), lambda l, j: (l, 0, j)),
        out_shape=jax.ShapeDtypeStruct((depth, b, n), F32),
        compiler_params=_cparams(("arbitrary", "arbitrary"), 32),
        name="ada_mod",
    )(c, ada_w, ada_b.reshape(depth, 1, n))


def _ffn_body(x_ref, mod_ref, lng_ref, lnb_ref, wg_ref, wu_ref, wo_ref, o_ref, *, s, order_in, order_out):
    if order_in == "cls":
        tm = x_ref.shape[0] * x_ref.shape[1]
        x = x_ref[...].reshape(tm, x_ref.shape[2])
    elif order_out == "cls":
        tm, d = x_ref.shape
        x = jnp.swapaxes(x_ref[...].reshape(tm // NCLS, NCLS, d), 0, 1).reshape(tm, d)
    else:
        tm = x_ref.shape[0]
        x = x_ref[...]
    h = _modulate(x, mod_ref, s).astype(BF16)
    g = jnp.dot(h, wg_ref[...], preferred_element_type=F32)
    u = jnp.dot(h, wu_ref[...], preferred_element_type=F32)
    a = (g * jax.nn.sigmoid(g) * u).astype(BF16)
    out = jnp.dot(a, wo_ref[...], preferred_element_type=F32)
    y = _post(x, out, mod_ref, s, 0.5, lng_ref, lnb_ref)
    if order_out == "cls":
        o_ref[...] = y.reshape(o_ref.shape)
    elif order_in == "cls":
        o_ref[...] = jnp.swapaxes(y.reshape(NCLS, tm // NCLS, y.shape[1]), 0, 1).reshape(o_ref.shape)
    else:
        o_ref[...] = y


def _ffn(x, mod, lng, lnb, w_in, w_out, s, order_in="nat", order_out="nat", tm=512):
    d = x.shape[-1]
    b = x.shape[0]
    sq = x.shape[1] if order_in == "nat" else x.shape[1] * x.shape[2]
    wg = w_in[:, :D_FF].astype(BF16)
    wu = w_in[:, D_FF:].astype(BF16)
    wo = w_out.astype(BF16)
    nat = pl.BlockSpec((None, tm, d), lambda i, j: (i, j, 0))
    cls = pl.BlockSpec((None, NCLS, tm // NCLS, d), lambda i, j: (i, 0, j, 0))
    out_shape = (b, sq, d) if order_out == "nat" else (b, NCLS, sq // NCLS, d)
    return pl.pallas_call(
        functools.partial(_ffn_body, s=s, order_in=order_in, order_out=order_out),
        grid=(b, sq // tm),
        in_specs=[cls if order_in == "cls" else nat,
                  pl.BlockSpec((None, 3 * N_SUB, d), lambda i, j: (i, 0, 0)),
                  _resident((1, d)), _resident((1, d)),
                  _resident(wg.shape), _resident(wu.shape), _resident(wo.shape)],
        out_specs=cls if order_out == "cls" else nat,
        out_shape=jax.ShapeDtypeStruct(out_shape, F32),
        compiler_params=_cparams(("parallel", "parallel"), 56),
        name=f"ffn_s{s}_{order_in}_{order_out}",
    )(x, mod, lng.reshape(1, d), lnb.reshape(1, d), wg, wu, wo)


def _mproj_body(x_ref, mod_ref, w_ref, wgate_ref, gbias_ref, convw_ref,
                q_ref, kt_ref, v_ref, o_ref, bc_ref, cm_ref, gt_ref, pbuf):
    tm, d = x_ref.shape
    nh, dk = MLSTM_HEADS, MLSTM_DQK
    L = CELL_CHUNK
    j = pl.program_id(1)

    @pl.when(j == 0)
    def _():
        pbuf[0:8, :] = jnp.zeros((8, d), F32)

    @pl.when(j > 0)
    def _():
        pbuf[0:8, :] = pbuf[tm:tm + 8, :]

    h = _modulate(x_ref[...], mod_ref, 1).astype(BF16)
    p_qk = jnp.dot(h, w_ref[:, :d], preferred_element_type=F32)
    gg = jnp.dot(h, wgate_ref[...], preferred_element_type=F32) + gbias_ref[...]
    v_ref[...] = jnp.dot(h, w_ref[:, d:2 * d], preferred_element_type=F32).astype(BF16)

    pbuf[8:tm + 8, :] = p_qk
    last = CONV_WIDTH - 1
    conv = convw_ref[last:last + 1, :] * pbuf[8:8 + tm, :]
    for i in range(last - 1, -1, -1):
        conv = conv + convw_ref[i:i + 1, :] * pbuf[8 - last + i:8 - last + i + tm, :]
    qk = conv * jax.nn.sigmoid(conv)
    lane = lax.broadcasted_iota(jnp.int32, (tm, LANES), 1)
    for hd in range(nh):
        pair = qk[:, (hd // 2) * LANES:(hd // 2 + 1) * LANES] * (dk ** -0.5)
        keep = (lane >= (hd % 2) * dk) & (lane < (hd % 2 + 1) * dk)
        q_ref[:, hd * LANES:(hd + 1) * LANES] = jnp.where(keep, pair, 0.0).astype(BF16)
    kt_ref[...] = qk[:, nh * dk:].T.astype(BF16)

    o_ref[...] = jnp.dot(h, w_ref[:, 2 * d:], preferred_element_type=F32).astype(BF16)

    ig = gg[:, :LANES]
    fp = gg[:, LANES:]
    logf = jnp.minimum(fp, 0.0) - jnp.log1p(jnp.exp(-jnp.abs(fp)))
    ri = lax.broadcasted_iota(jnp.int32, (L, L), 0)
    ci = lax.broadcasted_iota(jnp.int32, (L, L), 1)
    tril = jnp.where(ri >= ci, 1.0, 0.0).astype(F32)
    b = jnp.concatenate(
        [jnp.dot(tril, logf[c * L:(c + 1) * L, :], precision=lax.Precision.HIGHEST, preferred_element_type=F32)
         for c in range(tm // L)], axis=0)
    g = ig - b
    row_in_chunk = lax.broadcasted_iota(jnp.int32, (tm, LANES), 0) & (L - 1)
    cm = g
    shift = 1
    while shift < L:
        cm = jnp.maximum(cm, jnp.where(row_in_chunk >= shift, pltpu.roll(cm, shift, 0), -jnp.inf))
        shift *= 2
    bc_ref[...] = b
    cm_ref[...] = cm
    gt_ref[...] = g.T[:nh, :]


def _mlstm_proj(x, mod, w_in, gate_bias, conv_w, tm=512):
    b, sq, d = x.shape
    n_main = 3 * d
    n_gate = 2 * MLSTM_HEADS
    nk = MLSTM_HEADS * MLSTM_DQK
    w = w_in[:, :n_main].astype(BF16)
    nh = MLSTM_HEADS
    pad = ((0, 0), (0, LANES - nh))
    wgate = jnp.concatenate([jnp.pad(w_in[:, n_main:n_main + nh], pad),
                             jnp.pad(w_in[:, n_main + nh:], pad)], axis=1).astype(BF16)
    gb = gate_bias.astype(F32).reshape(1, n_gate)
    gbias = jnp.concatenate([jnp.pad(gb[:, :nh], pad), jnp.pad(gb[:, nh:], pad)], axis=1)
    row = pl.BlockSpec((None, tm, d), lambda i, j: (i, j, 0))
    lrow = pl.BlockSpec((None, tm, LANES), lambda i, j: (i, j, 0))
    act = jax.ShapeDtypeStruct((b, sq, d), BF16)
    gate_tile = jax.ShapeDtypeStruct((b, sq, LANES), F32)
    return pl.pallas_call(
        _mproj_body,
        grid=(b, sq // tm),
        in_specs=[row,
                  pl.BlockSpec((None, 3 * N_SUB, d), lambda i, j: (i, 0, 0)),
                  _resident(w.shape), _resident(wgate.shape), _resident(gbias.shape),
                  _resident((CONV_WIDTH, d))],
        out_specs=[row,
                   pl.BlockSpec((None, nk, tm), lambda i, j: (i, 0, j)),
                   row, row, lrow, lrow,
                   pl.BlockSpec((None, nh, tm), lambda i, j: (i, 0, j))],
        out_shape=[act, jax.ShapeDtypeStruct((b, nk, sq), BF16), act, act, gate_tile, gate_tile,
                   jax.ShapeDtypeStruct((b, nh, sq), F32)],
        scratch_shapes=[pltpu.VMEM((tm + 8, d), F32)],
        compiler_params=_cparams(("parallel", "arbitrary"), 48),
        name="mlstm_proj",
    )(x, mod, w, wgate, gbias, conv_w.astype(F32))


def _cell_body(q_ref, kt_ref, v_ref, o_ref, bc_ref, cm_ref, gt_ref, gain_ref, y_ref, cst, mst):
    @pl.when(pl.program_id(1) == 0)
    def _():
        cst[...] = jnp.zeros(cst.shape, F32)
        mst[...] = jnp.zeros(mst.shape, F32)

    for s in range(q_ref.shape[0] // CELL_CHUNK):
        _cell_chunk(s, q_ref, kt_ref, v_ref, o_ref, bc_ref, cm_ref, gt_ref, gain_ref, y_ref, cst, mst)


def _cell_chunk(s, q_ref, kt_ref, v_ref, o_ref, bc_ref, cm_ref, gt_ref, gain_ref, y_ref, cst, mst):
    L = CELL_CHUNK
    rows = slice(s * L, (s + 1) * L)
    nh, dk, dv = MLSTM_HEADS, MLSTM_DQK, MLSTM_DV
    heads = range(nh)

    m_prev = mst[...]
    bt = bc_ref[rows, :]
    cmt = cm_ref[rows, :]
    mt = jnp.maximum(m_prev, cmt)
    iwt = jnp.exp(m_prev - mt)
    emt = jnp.exp(-(bt + mt))
    b_last = bt[L - 1:L, :]
    cm_last = cmt[L - 1:L, :]
    m_loc = b_last + cm_last
    m_new = jnp.maximum(b_last + m_prev, m_loc)
    sp = jnp.exp(b_last + m_prev - m_new)
    sl = jnp.exp(m_loc - m_new)

    ri = lax.broadcasted_iota(jnp.int32, (L, L), 0)
    ci = lax.broadcasted_iota(jnp.int32, (L, L), 1)
    causal = ri >= ci
    ones = jnp.ones((L, LANES), BF16)
    kt_p = [kt_ref[p * LANES:(p + 1) * LANES, rows] for p in range(nh // 2)]
    outs = {}
    for h0 in range(0, nh, CELL_HEADS_PER_STAGE):
        heads = range(h0, h0 + CELL_HEADS_PER_STAGE)
        g_row = {h: gt_ref[h:h + 1, rows] for h in heads}
        m_col = {h: mt[:, h:h + 1] for h in heads}
        dmat = {h: jnp.exp(jnp.where(causal, g_row[h] - m_col[h], -jnp.inf)) for h in heads}
        wa = {h: jnp.exp(g_row[h] - cm_last[:, h:h + 1]) for h in heads}

        q_h = {h: q_ref[rows, h * LANES:(h + 1) * LANES] for h in heads}
        v_aug = {h: jnp.concatenate([v_ref[rows, h * dv:(h + 1) * dv], ones], axis=1) for h in heads}
        c_pair = {p: jnp.concatenate([cst[2 * p], cst[2 * p + 1]], axis=0).astype(BF16)
                  for p in range(h0 // 2, (h0 + CELL_HEADS_PER_STAGE) // 2)}

        scores = {h: jnp.dot(q_h[h], kt_p[h // 2], preferred_element_type=F32) for h in heads}
        inter = {h: jnp.dot(q_h[h], c_pair[h // 2], preferred_element_type=F32) for h in heads}
        pm = {h: (scores[h] * dmat[h]).astype(BF16) for h in heads}
        iw = {h: iwt[:, h:h + 1] for h in heads}
        num_aug = {h: jnp.dot(pm[h], v_aug[h], preferred_element_type=F32) + iw[h] * inter[h] for h in heads}

        em = {h: emt[:, h:h + 1] for h in heads}
        hh = {h: num_aug[h][:, :dv] / jnp.maximum(jnp.abs(num_aug[h][:, dv:]), em[h]) for h in heads}
        ms = {h: jnp.mean(hh[h] * hh[h], axis=-1, keepdims=True) for h in heads}
        rs = {h: lax.rsqrt(ms[h] + RMS_EPS) for h in heads}
        hn = {h: hh[h] * rs[h] * gain_ref[:, h * dv:(h + 1) * dv] for h in heads}
        og = {h: o_ref[rows, h * dv:(h + 1) * dv].astype(F32) for h in heads}
        outs.update({h: (jax.nn.sigmoid(og[h]) * hn[h]).astype(BF16) for h in heads})

        kw = {h: (kt_p[h // 2][(h % 2) * dk:(h % 2 + 1) * dk, :].astype(F32) * wa[h]).astype(BF16) for h in heads}
        c_loc = {h: jnp.dot(kw[h], v_aug[h], preferred_element_type=F32) for h in heads}
        for h in heads:
            cst[h] = sp[:, h:h + 1] * cst[h] + sl[:, h:h + 1] * c_loc[h]
    y_ref[rows, :] = jnp.concatenate([outs[h] for h in range(nh)], axis=1)
    mst[...] = m_new


def _mlstm_cell(q, kt, v, o, bc, cm, gt, head_gain):
    b, sq, d = v.shape
    L = CELL_CHUNKS_PER_STEP * CELL_CHUNK
    nk = kt.shape[1]
    row = pl.BlockSpec((None, L, d), lambda i, j: (i, j, 0))
    lrow = pl.BlockSpec((None, L, LANES), lambda i, j: (i, j, 0))
    return pl.pallas_call(
        _cell_body,
        grid=(b, sq // L),
        in_specs=[row,
                  pl.BlockSpec((None, nk, L), lambda i, j: (i, 0, j)),
                  row, row, lrow, lrow,
                  pl.BlockSpec((None, MLSTM_HEADS, L), lambda i, j: (i, 0, j)),
                  _resident((1, d))],
        out_specs=row,
        out_shape=jax.ShapeDtypeStruct((b, sq, d), BF16),
        scratch_shapes=[pltpu.VMEM((MLSTM_HEADS, MLSTM_DQK, 2 * MLSTM_DV), F32),
                        pltpu.VMEM((1, LANES), F32)],
        compiler_params=_cparams(("parallel", "arbitrary"), 32),
        name="mlstm_cell",
    )(q, kt, v, o, bc, cm, gt, head_gain.astype(F32).reshape(1, d))


def _mixer_out_body(y_ref, x_ref, mod_ref, lng_ref, lnb_ref, w_ref, o_ref):
    out = jnp.dot(y_ref[...], w_ref[...], preferred_element_type=F32)
    o_ref[...] = _post(x_ref[...], out, mod_ref, 1, 1.0, lng_ref, lnb_ref)


def _mixer_out(y, x, mod, lng, lnb, w_out, tm=1024):
    b, sq, d = x.shape
    w = w_out.astype(BF16)
    row = pl.BlockSpec((None, tm, d), lambda i, j: (i, j, 0))
    return pl.pallas_call(
        _mixer_out_body,
        grid=(b, sq // tm),
        in_specs=[row, row,
                  pl.BlockSpec((None, 3 * N_SUB, d), lambda i, j: (i, 0, 0)),
                  _resident((1, d)), _resident((1, d)), _resident(w.shape)],
        out_specs=row,
        out_shape=jax.ShapeDtypeStruct(x.shape, F32),
        compiler_params=_cparams(("parallel", "parallel"), 48),
        name="mlstm_out",
    )(y, x, mod, lng.reshape(1, d), lnb.reshape(1, d), w)


def _aproj_body(x_ref, mod_ref, w_ref, o_ref):
    h = _modulate(x_ref[...], mod_ref, 1).astype(BF16)
    o_ref[...] = jnp.dot(h, w_ref[...], preferred_element_type=F32).astype(BF16)


def _attn_proj(x, mod, w_bf16, tm=1024, tn=3072):
    b, sq, d = x.shape
    n = w_bf16.shape[1]
    return pl.pallas_call(
        _aproj_body,
        grid=(n // tn, b, sq // tm),
        in_specs=[pl.BlockSpec((None, tm, d), lambda g, i, j: (i, j, 0)),
                  pl.BlockSpec((None, 3 * N_SUB, d), lambda g, i, j: (i, 0, 0)),
                  pl.BlockSpec((d, tn), lambda g, i, j: (0, g))],
        out_specs=pl.BlockSpec((None, tm, tn), lambda g, i, j: (i, j, g)),
        out_shape=jax.ShapeDtypeStruct((b, sq, n), BF16),
        compiler_params=_cparams(("arbitrary", "arbitrary", "arbitrary"), 56),
        name="attn_proj",
    )(x, mod, w_bf16)


def _flat_rows(v):
    return v.reshape(-1, v.shape[-1])


def _attn_block(q, kc, vc, kp, vp, bias):
    has_prev = kp is not None
    dh = ATTN_HEAD_DIM
    cols = [slice(h * dh, (h + 1) * dh) for h in range(ATTN_HEADS)]
    qb = q.shape[0]
    pb = kp.shape[0] if has_prev else 0
    lane = lax.broadcasted_iota(jnp.int32, (qb, LANES), 1)
    nt = (((1,), (1,)), ((), ()))

    outs, lses = [], []
    stage = ATT_HEADS_PER_STAGE_NO_PREV if kp is None else ATT_HEADS_PER_STAGE_PREV
    for h0 in range(0, ATTN_HEADS, stage):
        heads = range(h0, h0 + stage)
        scores = {}
        for h in heads:
            s = lax.dot_general(q[:, cols[h]], kc[:, cols[h]], nt, preferred_element_type=F32)
            if has_prev:
                sp = lax.dot_general(q[:, cols[h]], kp[:, cols[h]], nt, preferred_element_type=F32)
                s = jnp.concatenate([sp, s], axis=1)
            scores[h] = s
        probs, maxes = {}, {}
        for h in heads:
            t = scores[h] * (dh ** -0.5 * LOG2E) + bias
            maxes[h] = jnp.max(t, axis=1, keepdims=True)
            probs[h] = jnp.exp2(t - maxes[h]).astype(BF16)
        accs = {}
        for h in heads:
            acc = jnp.dot(probs[h][:, pb:], jnp.concatenate([vc[:, cols[h]], jnp.ones((qb, LANES), BF16)], axis=1),
                          preferred_element_type=F32)
            if has_prev:
                acc = acc + jnp.dot(probs[h][:, :pb],
                                    jnp.concatenate([vp[:, cols[h]], jnp.ones((pb, LANES), BF16)], axis=1),
                                    preferred_element_type=F32)
            accs[h] = acc
        outs += [(accs[h][:, :dh] / accs[h][:, dh:]).astype(BF16) for h in heads]
        lses += [maxes[h] * LN2 + jnp.log(accs[h][:, dh:]) for h in heads]
    lse_tile = lses[0]
    for h in range(1, ATTN_HEADS):
        lse_tile = jnp.where(lane == h, lses[h], lse_tile)
    return jnp.concatenate(outs, axis=1), lse_tile


def _attn_whole_body(q_ref, k_ref, v_ref, bias_ref, o_ref, lse_ref):
    for c in range(q_ref.shape[0]):
        o, lse = _attn_block(q_ref[c], k_ref[c], v_ref[c], None, None, bias_ref[1])
        o_ref[c] = o
        lse_ref[c] = lse


def _attn_chain_body(q_ref, k_ref, v_ref, kp_ref, vp_ref, bias_ref, o_ref, lse_ref, *, n_axis):
    n = pl.program_id(n_axis)
    ur = q_ref.shape[1] // 2
    urp = kp_ref.shape[1]

    def part(ref, lo, hi):
        return _flat_rows(ref[:, lo:hi, :])

    o_a, l_a = _attn_block(part(q_ref, 0, ur), part(k_ref, 0, ur), part(v_ref, 0, ur),
                           _flat_rows(kp_ref[...]), _flat_rows(vp_ref[...]), bias_ref[jnp.minimum(n, 1)])
    o_b, l_b = _attn_block(part(q_ref, ur, 2 * ur), part(k_ref, ur, 2 * ur), part(v_ref, ur, 2 * ur),
                           part(k_ref, ur - urp, ur), part(v_ref, ur - urp, ur), bias_ref[1])
    nslab = q_ref.shape[0]
    o_ref[:, 0:ur, :] = o_a.reshape(nslab, ur, o_a.shape[-1])
    o_ref[:, ur:2 * ur, :] = o_b.reshape(nslab, ur, o_b.shape[-1])
    lse_ref[:, 0:ur, :] = l_a.reshape(nslab, ur, LANES)
    lse_ref[:, ur:2 * ur, :] = l_b.reshape(nslab, ur, LANES)


def _window_bias(nslab, cur_rows, prev_rows):
    qb = nslab * cur_rows
    pos_c = np.array([nslab * (i % cur_rows) + i // cur_rows for i in range(qb)])
    pb = nslab * prev_rows
    pos_p = np.array([nslab * (i % prev_rows) + i // prev_rows for i in range(pb)], dtype=np.int64) - pb
    kpos = np.concatenate([pos_p, pos_c])
    delta = pos_c[:, None] - kpos[None, :]
    ok = (delta >= 0) & (delta <= ATT_SPAN)
    later = np.where(ok, 0.0, -np.inf).astype(np.float32)
    first = later.copy()
    first[:, :pb] = -np.inf
    return jnp.asarray(np.stack([first, later]))


def _attn(proj, g, dil, d):
    b, ncls, u, _ = proj.shape
    qb = ATT_QB
    cq, ck, cv = 3 * g, 3 * g + 1, 3 * g + 2
    nsub = ATT_BLOCKS_PER_STEP
    if dil == NCLS:
        nsub = ATT_SEQS_PER_STEP
        assert u == qb
        bias = _window_bias(1, qb, 0)
        arrs = (proj,) * 3
        def blk(col, width=d):
            return pl.BlockSpec((None, nsub, qb, width), lambda i, r: (i, r, 0, col))
        in_specs = [blk(cq), blk(ck), blk(cv)]
        o_spec, l_spec = blk(0), blk(0, LANES)
        grid = (b, ncls // nsub)
        o_shape, l_shape = (b, ncls, u, d), (b, ncls, u, LANES)
        sem = ("parallel", "parallel")
        body = _attn_whole_body
    else:
        nslab = ncls // dil
        ur = qb // nslab
        urp = max(ATT_SPAN // nslab, 16)
        bias = _window_bias(nslab, ur, urp)
        p5 = proj.reshape(b, nslab, dil, u, proj.shape[-1])
        arrs = (p5,) * 5
        step = nsub * ur
        def blk(col, width=d):
            return pl.BlockSpec((None, nslab, None, step, width), lambda i, r, n: (i, 0, r, n, col))
        def prev(col):
            return pl.BlockSpec((None, nslab, None, urp, d),
                                lambda i, r, n: (i, 0, r, jnp.maximum((step // urp) * n - 1, 0), col))
        in_specs = [blk(cq), blk(ck), blk(cv), prev(ck), prev(cv)]
        o_spec, l_spec = blk(0), blk(0, LANES)
        grid = (b, dil, u // step)
        o_shape, l_shape = (b, nslab, dil, u, d), (b, nslab, dil, u, LANES)
        sem = ("parallel", "parallel", "arbitrary")
        body = functools.partial(_attn_chain_body, n_axis=2)
    bias_spec = pl.BlockSpec(bias.shape, lambda *_: (0, 0, 0))
    o, lse = pl.pallas_call(
        body,
        grid=grid,
        in_specs=in_specs + [bias_spec],
        out_specs=[o_spec, l_spec],
        out_shape=[jax.ShapeDtypeStruct(o_shape, BF16), jax.ShapeDtypeStruct(l_shape, F32)],
        compiler_params=_cparams(sem, 40),
        name=f"attn_d{dil}",
    )(*arrs, bias)
    return o.reshape(b, ncls * u, d), lse.reshape(b, ncls * u, LANES)


def _amerge_body(o1_ref, o4_ref, o16_ref, l1_ref, l4_ref, l16_ref, x_ref, mod_ref, lng_ref, lnb_ref,
                 w_ref, out_ref):
    dh = ATTN_HEAD_DIM
    heads = range(ATTN_HEADS)
    l1, l4, l16 = l1_ref[...], l4_ref[...], l16_ref[...]
    mx = jnp.maximum(jnp.maximum(l1, l4), l16)
    e1, e4, e16 = jnp.exp(l1 - mx), jnp.exp(l4 - mx), jnp.exp(l16 - mx)
    tot = e1 + e4 + e16
    w1 = e1 / tot
    w4 = e4 / tot
    w1_h = [w1[:, h:h + 1] for h in heads]
    w4_h = [w4[:, h:h + 1] for h in heads]
    ys = []
    for h in heads:
        cs = slice(h * dh, (h + 1) * dh)
        o16 = o16_ref[:, cs].astype(F32)
        acc = o16 + w1_h[h] * (o1_ref[:, cs].astype(F32) - o16) + w4_h[h] * (o4_ref[:, cs].astype(F32) - o16)
        ys.append(acc.astype(BF16))
    out = jnp.dot(jnp.concatenate(ys, axis=1), w_ref[...], preferred_element_type=F32)
    out_ref[...] = _post(x_ref[...], out, mod_ref, 1, 1.0, lng_ref, lnb_ref)


def _attn_merge_out(o_list, lse_list, x, mod, lng, lnb, w_out, tm=1024):
    b, sq, d = x.shape
    w = w_out.astype(BF16)
    row = pl.BlockSpec((None, tm, d), lambda i, j: (i, j, 0))
    lrow = pl.BlockSpec((None, tm, LANES), lambda i, j: (i, j, 0))
    return pl.pallas_call(
        _amerge_body,
        grid=(b, sq // tm),
        in_specs=[row, row, row, lrow, lrow, lrow, row,
                  pl.BlockSpec((None, 3 * N_SUB, d), lambda i, j: (i, 0, 0)),
                  _resident((1, d)), _resident((1, d)), _resident(w.shape)],
        out_specs=row,
        out_shape=jax.ShapeDtypeStruct(x.shape, F32),
        compiler_params=_cparams(("parallel", "parallel"), 56),
        name="attn_merge_out",
    )(*o_list, *lse_list, x, mod, lng.reshape(1, d), lnb.reshape(1, d), w)


def kernel(x, c, ada_w, ada_b, ln_g, ln_b, ffn_w_in, ffn_w_out, mlstm_w_in, mlstm_gate_bias, mlstm_conv_w,
           mlstm_head_gain, mlstm_w_out, attn_w_in, attn_w_out):
    b, sq, d = x.shape
    assert all(w // dl == ATT_SPAN for w, dl in DIL_GROUPS) and [dl for _, dl in DIL_GROUPS] == [1, 4, NCLS]
    assert sq == NCLS * ATT_QB and d == MLSTM_HEADS * MLSTM_DV == ATTN_HEADS * ATTN_HEAD_DIM
    mod_all = _ada(c, ada_w, ada_b)
    for layer in range(DEPTH):
        mod = mod_all[layer].reshape(b, 3 * N_SUB, d)
        j = layer // 2
        if layer % 2 == 0:
            x = _ffn(x, mod, ln_g[layer, 0], ln_b[layer, 0], ffn_w_in[layer, 0], ffn_w_out[layer, 0], 0)
            q, kt, v, o, bc, cm, gt = _mlstm_proj(x, mod, mlstm_w_in[j], mlstm_gate_bias[j], mlstm_conv_w[j])
            y = _mlstm_cell(q, kt, v, o, bc, cm, gt, mlstm_head_gain[j])
            x = _mixer_out(y, x, mod, ln_g[layer, 1], ln_b[layer, 1], mlstm_w_out[j])
            x = _ffn(x, mod, ln_g[layer, 2], ln_b[layer, 2], ffn_w_in[layer, 1], ffn_w_out[layer, 1], 2)
        else:
            xc = _ffn(x, mod, ln_g[layer, 0], ln_b[layer, 0], ffn_w_in[layer, 0], ffn_w_out[layer, 0], 0,
                      order_out="cls")
            xf = xc.reshape(b, sq, d)
            proj = _attn_proj(xf, mod, attn_w_in[j].astype(BF16)).reshape(b, NCLS, sq // NCLS, -1)
            outs, lses = [], []
            for g, (_, dil) in enumerate(DIL_GROUPS):
                o_g, lse_g = _attn(proj, g, dil, d)
                outs.append(o_g)
                lses.append(lse_g)
            xf = _attn_merge_out(outs, lses, xf, mod, ln_g[layer, 1], ln_b[layer, 1], attn_w_out[j])
            x = _ffn(xf.reshape(b, NCLS, sq // NCLS, d), mod, ln_g[layer, 2], ln_b[layer, 2],
                     ffn_w_in[layer, 1], ffn_w_out[layer, 1], 2, order_in="cls")
    return x
```

```python
import functools

import numpy as np
import jax
import jax.numpy as jnp
from jax import lax
from jax.experimental import pallas as pl
from jax.experimental.pallas import tpu as pltpu

F32 = jnp.float32
BF16 = jnp.bfloat16

DEPTH = 2
N_SUB = 3
D_FF = 2816
MLSTM_HEADS = 8
MLSTM_DQK = 64
MLSTM_DV = 128
CONV_WIDTH = 4
DIL_GROUPS = ((128, 1), (512, 4), (2048, 16))
ATTN_HEADS = 8
ATTN_HEAD_DIM = 128
ALPHA = (2 * DEPTH) ** 0.25
LN_EPS = 1e-5
RMS_EPS = 1e-6
LOG2E = 1.4426950408889634
LN2 = 0.6931471805599453

LANES = 128
NCLS = 16
CELL_CHUNK = 128
CELL_CHUNKS_PER_STEP = 4
FFN_SUBTILE_ROWS = 512
CELL_HEADS_PER_STAGE = 8
ATT_SPAN = 128
ATT_QB = 256
ATT_BLOCKS_PER_STEP = 4
ATT_SEQS_PER_STEP = 4
ATT_HEADS_PER_STAGE_NO_PREV = 4
ATT_HEADS_PER_STAGE_PREV = 8
MIB = 1024 * 1024


def _cparams(sem, vmem_mib):
    return pltpu.CompilerParams(dimension_semantics=sem, vmem_limit_bytes=int(vmem_mib * MIB))


def _resident(shape):
    nd = len(shape)
    return pl.BlockSpec(shape, lambda *_: (0,) * nd, pipeline_mode=pl.Buffered(1))


def _layer_norm(y, g, b):
    mu = jnp.mean(y, axis=-1, keepdims=True)
    yc = y - mu
    var = jnp.mean(yc * yc, axis=-1, keepdims=True)
    return yc * lax.rsqrt(var + LN_EPS) * g + b


def _modulate(x, mod_ref, s):
    return x * (1.0 + mod_ref[3 * s + 1:3 * s + 2, :]) + mod_ref[3 * s:3 * s + 1, :]


def _post(x, out, mod_ref, s, weight, lng_ref, lnb_ref):
    y = ALPHA * x + (weight * (1.0 + mod_ref[3 * s + 2:3 * s + 3, :])) * out
    return _layer_norm(y, lng_ref[...], lnb_ref[...])


def _ada_body(c_ref, w_ref, b_ref, o_ref):
    c = c_ref[...]
    cond = c * jax.nn.sigmoid(c)
    o_ref[...] = jnp.dot(cond, w_ref[...], precision=lax.Precision.HIGHEST,
                         preferred_element_type=F32) + b_ref[...]


def _ada(c, ada_w, ada_b):
    depth, d, n = ada_w.shape
    b = c.shape[0]
    tn = 1152
    return pl.pallas_call(
        _ada_body,
        grid=(depth, n // tn),
        in_specs=[pl.BlockSpec((b, d), lambda l, j: (0, 0)),
                  pl.BlockSpec((None, d, tn), lambda l, j: (l, 0, j)),
                  pl.BlockSpec((None, 1, tn), lambda l, j: (l, 0, j))],
        out_specs=pl.BlockSpec((None, b, tn), lambda l, j: (l, 0, j)),
        out_shape=jax.ShapeDtypeStruct((depth, b, n), F32),
        compiler_params=_cparams(("arbitrary", "arbitrary"), 32),
        name="ada_mod",
    )(c, ada_w, ada_b.reshape(depth, 1, n))


def _ffn_body(x_ref, mod_ref, lng_ref, lnb_ref, wg_ref, wu_ref, wo_ref, o_ref, *, s, order_in, order_out):
    st = FFN_SUBTILE_ROWS
    per = st // NCLS
    d = x_ref.shape[-1]
    tm = x_ref.shape[0] * x_ref.shape[1] if order_in == "cls" else x_ref.shape[0]
    for k in range(tm // st):
        nat_rows = slice(k * st, (k + 1) * st)
        cls_rows = slice(k * per, (k + 1) * per)
        if order_in == "cls":
            x = x_ref[:, cls_rows, :].reshape(st, d)
        elif order_out == "cls":
            x = jnp.swapaxes(x_ref[nat_rows, :].reshape(per, NCLS, d), 0, 1).reshape(st, d)
        else:
            x = x_ref[nat_rows, :]
        h = _modulate(x, mod_ref, s).astype(BF16)
        g = jnp.dot(h, wg_ref[...], preferred_element_type=F32)
        u = jnp.dot(h, wu_ref[...], preferred_element_type=F32)
        a = (g * jax.nn.sigmoid(g) * u).astype(BF16)
        out = jnp.dot(a, wo_ref[...], preferred_element_type=F32)
        y = _post(x, out, mod_ref, s, 0.5, lng_ref, lnb_ref)
        if order_out == "cls":
            o_ref[:, cls_rows, :] = y.reshape(NCLS, per, d)
        elif order_in == "cls":
            o_ref[nat_rows, :] = jnp.swapaxes(y.reshape(NCLS, per, d), 0, 1).reshape(st, d)
        else:
            o_ref[nat_rows, :] = y


def _ffn(x, mod, lng, lnb, w_in, w_out, s, order_in="nat", order_out="nat", tm=1024):
    d = x.shape[-1]
    b = x.shape[0]
    sq = x.shape[1] if order_in == "nat" else x.shape[1] * x.shape[2]
    wg = w_in[:, :D_FF].astype(BF16)
    wu = w_in[:, D_FF:].astype(BF16)
    wo = w_out.astype(BF16)
    nat = pl.BlockSpec((None, tm, d), lambda i, j: (i, j, 0))
    cls = pl.BlockSpec((None, NCLS, tm // NCLS, d), lambda i, j: (i, 0, j, 0))
    out_shape = (b, sq, d) if order_out == "nat" else (b, NCLS, sq // NCLS, d)
    return pl.pallas_call(
        functools.partial(_ffn_body, s=s, order_in=order_in, order_out=order_out),
        grid=(b, sq // tm),
        in_specs=[cls if order_in == "cls" else nat,
                  pl.BlockSpec((None, 3 * N_SUB, d), lambda i, j: (i, 0, 0)),
                  _resident((1, d)), _resident((1, d)),
                  _resident(wg.shape), _resident(wu.shape), _resident(wo.shape)],
        out_specs=cls if order_out == "cls" else nat,
        out_shape=jax.ShapeDtypeStruct(out_shape, F32),
        compiler_params=_cparams(("parallel", "parallel"), 56),
        name=f"ffn_s{s}_{order_in}_{order_out}",
    )(x, mod, lng.reshape(1, d), lnb.reshape(1, d), wg, wu, wo)


def _mproj_body(x_ref, mod_ref, w_ref, wgate_ref, gbias_ref, convw_ref,
                q_ref, kt_ref, v_ref, o_ref, bc_ref, cm_ref, gt_ref, pbuf):
    tm, d = x_ref.shape
    nh, dk = MLSTM_HEADS, MLSTM_DQK
    L = CELL_CHUNK
    j = pl.program_id(1)

    @pl.when(j == 0)
    def _():
        pbuf[0:8, :] = jnp.zeros((8, d), F32)

    @pl.when(j > 0)
    def _():
        pbuf[0:8, :] = pbuf[tm:tm + 8, :]

    h = _modulate(x_ref[...], mod_ref, 1).astype(BF16)
    p_qk = jnp.dot(h, w_ref[:, :d], preferred_element_type=F32)
    gg = jnp.dot(h, wgate_ref[...], preferred_element_type=F32) + gbias_ref[...]
    v_ref[...] = jnp.dot(h, w_ref[:, d:2 * d], preferred_element_type=F32).astype(BF16)

    pbuf[8:tm + 8, :] = p_qk
    last = CONV_WIDTH - 1
    conv = convw_ref[last:last + 1, :] * pbuf[8:8 + tm, :]
    for i in range(last - 1, -1, -1):
        conv = conv + convw_ref[i:i + 1, :] * pbuf[8 - last + i:8 - last + i + tm, :]
    qk = conv * jax.nn.sigmoid(conv)
    lane = lax.broadcasted_iota(jnp.int32, (tm, LANES), 1)
    for hd in range(nh):
        pair = qk[:, (hd // 2) * LANES:(hd // 2 + 1) * LANES] * (dk ** -0.5)
        keep = (lane >= (hd % 2) * dk) & (lane < (hd % 2 + 1) * dk)
        q_ref[:, hd * LANES:(hd + 1) * LANES] = jnp.where(keep, pair, 0.0).astype(BF16)
    kt_ref[...] = qk[:, nh * dk:].T.astype(BF16)

    o_ref[...] = jnp.dot(h, w_ref[:, 2 * d:], preferred_element_type=F32).astype(BF16)

    ig = gg[:, :LANES]
    fp = gg[:, LANES:]
    logf = jnp.minimum(fp, 0.0) - jnp.log1p(jnp.exp(-jnp.abs(fp)))
    ri = lax.broadcasted_iota(jnp.int32, (L, L), 0)
    ci = lax.broadcasted_iota(jnp.int32, (L, L), 1)
    tril = jnp.where(ri >= ci, 1.0, 0.0).astype(F32)
    b = jnp.concatenate(
        [jnp.dot(tril, logf[c * L:(c + 1) * L, :], precision=lax.Precision.HIGHEST, preferred_element_type=F32)
         for c in range(tm // L)], axis=0)
    g = ig - b
    row_in_chunk = lax.broadcasted_iota(jnp.int32, (tm, LANES), 0) & (L - 1)
    cm = g
    shift = 1
    while shift < L:
        cm = jnp.maximum(cm, jnp.where(row_in_chunk >= shift, pltpu.roll(cm, shift, 0), -jnp.inf))
        shift *= 2
    bc_ref[...] = b
    cm_ref[...] = cm
    gt_ref[...] = g.T[:nh, :]


def _mlstm_proj(x, mod, w_in, gate_bias, conv_w, tm=1024):
    b, sq, d = x.shape
    n_main = 3 * d
    n_gate = 2 * MLSTM_HEADS
    nk = MLSTM_HEADS * MLSTM_DQK
    w = w_in[:, :n_main].astype(BF16)
    nh = MLSTM_HEADS
    pad = ((0, 0), (0, LANES - nh))
    wgate = jnp.concatenate([jnp.pad(w_in[:, n_main:n_main + nh], pad),
                             jnp.pad(w_in[:, n_main + nh:], pad)], axis=1).astype(BF16)
    gb = gate_bias.astype(F32).reshape(1, n_gate)
    gbias = jnp.concatenate([jnp.pad(gb[:, :nh], pad), jnp.pad(gb[:, nh:], pad)], axis=1)
    row = pl.BlockSpec((None, tm, d), lambda i, j: (i, j, 0))
    lrow = pl.BlockSpec((None, tm, LANES), lambda i, j: (i, j, 0))
    act = jax.ShapeDtypeStruct((b, sq, d), BF16)
    gate_tile = jax.ShapeDtypeStruct((b, sq, LANES), F32)
    return pl.pallas_call(
        _mproj_body,
        grid=(b, sq // tm),
        in_specs=[row,
                  pl.BlockSpec((None, 3 * N_SUB, d), lambda i, j: (i, 0, 0)),
                  _resident(w.shape), _resident(wgate.shape), _resident(gbias.shape),
                  _resident((CONV_WIDTH, d))],
        out_specs=[row,
                   pl.BlockSpec((None, nk, tm), lambda i, j: (i, 0, j)),
                   row, row, lrow, lrow,
                   pl.BlockSpec((None, nh, tm), lambda i, j: (i, 0, j))],
        out_shape=[act, jax.ShapeDtypeStruct((b, nk, sq), BF16), act, act, gate_tile, gate_tile,
                   jax.ShapeDtypeStruct((b, nh, sq), F32)],
        scratch_shapes=[pltpu.VMEM((tm + 8, d), F32)],
        compiler_params=_cparams(("parallel", "arbitrary"), 56),
        name="mlstm_proj",
    )(x, mod, w, wgate, gbias, conv_w.astype(F32))


def _cell_body(q_ref, kt_ref, v_ref, o_ref, bc_ref, cm_ref, gt_ref, gain_ref, y_ref, cst, mst):
    @pl.when(pl.program_id(1) == 0)
    def _():
        cst[...] = jnp.zeros(cst.shape, F32)
        mst[...] = jnp.zeros(mst.shape, F32)

    for s in range(q_ref.shape[0] // CELL_CHUNK):
        _cell_chunk(s, q_ref, kt_ref, v_ref, o_ref, bc_ref, cm_ref, gt_ref, gain_ref, y_ref, cst, mst)


def _cell_chunk(s, q_ref, kt_ref, v_ref, o_ref, bc_ref, cm_ref, gt_ref, gain_ref, y_ref, cst, mst):
    L = CELL_CHUNK
    rows = slice(s * L, (s + 1) * L)
    nh, dk, dv = MLSTM_HEADS, MLSTM_DQK, MLSTM_DV
    heads = range(nh)

    m_prev = mst[...]
    bt = bc_ref[rows, :]
    cmt = cm_ref[rows, :]
    mt = jnp.maximum(m_prev, cmt)
    iwt = jnp.exp(m_prev - mt)
    emt = jnp.exp(-(bt + mt))
    b_last = bt[L - 1:L, :]
    cm_last = cmt[L - 1:L, :]
    m_loc = b_last + cm_last
    m_new = jnp.maximum(b_last + m_prev, m_loc)
    sp = jnp.exp(b_last + m_prev - m_new)
    sl = jnp.exp(m_loc - m_new)

    ri = lax.broadcasted_iota(jnp.int32, (L, L), 0)
    ci = lax.broadcasted_iota(jnp.int32, (L, L), 1)
    causal = ri >= ci
    ones = jnp.ones((L, LANES), BF16)
    kt_p = [kt_ref[p * LANES:(p + 1) * LANES, rows] for p in range(nh // 2)]
    outs = {}
    for h0 in range(0, nh, CELL_HEADS_PER_STAGE):
        heads = range(h0, h0 + CELL_HEADS_PER_STAGE)
        g_row = {h: gt_ref[h:h + 1, rows] for h in heads}
        m_col = {h: mt[:, h:h + 1] for h in heads}
        dmat = {h: jnp.exp(jnp.where(causal, g_row[h] - m_col[h], -jnp.inf)) for h in heads}
        wa = {h: jnp.exp(g_row[h] - cm_last[:, h:h + 1]) for h in heads}

        q_h = {h: q_ref[rows, h * LANES:(h + 1) * LANES] for h in heads}
        v_aug = {h: jnp.concatenate([v_ref[rows, h * dv:(h + 1) * dv], ones], axis=1) for h in heads}
        c_pair = {p: jnp.concatenate([cst[2 * p], cst[2 * p + 1]], axis=0).astype(BF16)
                  for p in range(h0 // 2, (h0 + CELL_HEADS_PER_STAGE) // 2)}

        scores = {h: jnp.dot(q_h[h], kt_p[h // 2], preferred_element_type=F32) for h in heads}
        inter = {h: jnp.dot(q_h[h], c_pair[h // 2], preferred_element_type=F32) for h in heads}
        pm = {h: (scores[h] * dmat[h]).astype(BF16) for h in heads}
        iw = {h: iwt[:, h:h + 1] for h in heads}
        num_aug = {h: jnp.dot(pm[h], v_aug[h], preferred_element_type=F32) + iw[h] * inter[h] for h in heads}

        em = {h: emt[:, h:h + 1] for h in heads}
        hh = {h: num_aug[h][:, :dv] / jnp.maximum(jnp.abs(num_aug[h][:, dv:]), em[h]) for h in heads}
        ms = {h: jnp.mean(hh[h] * hh[h], axis=-1, keepdims=True) for h in heads}
        rs = {h: lax.rsqrt(ms[h] + RMS_EPS) for h in heads}
        hn = {h: hh[h] * rs[h] * gain_ref[:, h * dv:(h + 1) * dv] for h in heads}
        og = {h: o_ref[rows, h * dv:(h + 1) * dv].astype(F32) for h in heads}
        outs.update({h: (jax.nn.sigmoid(og[h]) * hn[h]).astype(BF16) for h in heads})

        kw = {h: (kt_p[h // 2][(h % 2) * dk:(h % 2 + 1) * dk, :].astype(F32) * wa[h]).astype(BF16) for h in heads}
        c_loc = {h: jnp.dot(kw[h], v_aug[h], preferred_element_type=F32) for h in heads}
        for h in heads:
            cst[h] = sp[:, h:h + 1] * cst[h] + sl[:, h:h + 1] * c_loc[h]
    y_ref[rows, :] = jnp.concatenate([outs[h] for h in range(nh)], axis=1)
    mst[...] = m_new


def _mlstm_cell(q, kt, v, o, bc, cm, gt, head_gain):
    b, sq, d = v.shape
    L = CELL_CHUNKS_PER_STEP * CELL_CHUNK
    nk = kt.shape[1]
    row = pl.BlockSpec((None, L, d), lambda i, j: (i, j, 0))
    lrow = pl.BlockSpec((None, L, LANES), lambda i, j: (i, j, 0))
    return pl.pallas_call(
        _cell_body,
        grid=(b, sq // L),
        in_specs=[row,
                  pl.BlockSpec((None, nk, L), lambda i, j: (i, 0, j)),
                  row, row, lrow, lrow,
                  pl.BlockSpec((None, MLSTM_HEADS, L), lambda i, j: (i, 0, j)),
                  _resident((1, d))],
        out_specs=row,
        out_shape=jax.ShapeDtypeStruct((b, sq, d), BF16),
        scratch_shapes=[pltpu.VMEM((MLSTM_HEADS, MLSTM_DQK, 2 * MLSTM_DV), F32),
                        pltpu.VMEM((1, LANES), F32)],
        compiler_params=_cparams(("parallel", "arbitrary"), 32),
        name="mlstm_cell",
    )(q, kt, v, o, bc, cm, gt, head_gain.astype(F32).reshape(1, d))


def _mixer_out_body(y_ref, x_ref, mod_ref, lng_ref, lnb_ref, w_ref, o_ref):
    out = jnp.dot(y_ref[...], w_ref[...], preferred_element_type=F32)
    o_ref[...] = _post(x_ref[...], out, mod_ref, 1, 1.0, lng_ref, lnb_ref)


def _mixer_out(y, x, mod, lng, lnb, w_out, tm=1024):
    b, sq, d = x.shape
    w = w_out.astype(BF16)
    row = pl.BlockSpec((None, tm, d), lambda i, j: (i, j, 0))
    return pl.pallas_call(
        _mixer_out_body,
        grid=(b, sq // tm),
        in_specs=[row, row,
                  pl.BlockSpec((None, 3 * N_SUB, d), lambda i, j: (i, 0, 0)),
                  _resident((1, d)), _resident((1, d)), _resident(w.shape)],
        out_specs=row,
        out_shape=jax.ShapeDtypeStruct(x.shape, F32),
        compiler_params=_cparams(("parallel", "parallel"), 48),
        name="mlstm_out",
    )(y, x, mod, lng.reshape(1, d), lnb.reshape(1, d), w)


def _aproj_body(x_ref, mod_ref, w_ref, o_ref):
    h = _modulate(x_ref[...], mod_ref, 1).astype(BF16)
    o_ref[...] = jnp.dot(h, w_ref[...], preferred_element_type=F32).astype(BF16)


def _attn_proj(x, mod, w_bf16, tm=1024, tn=3072):
    b, sq, d = x.shape
    n = w_bf16.shape[1]
    return pl.pallas_call(
        _aproj_body,
        grid=(n // tn, b, sq // tm),
        in_specs=[pl.BlockSpec((None, tm, d), lambda g, i, j: (i, j, 0)),
                  pl.BlockSpec((None, 3 * N_SUB, d), lambda g, i, j: (i, 0, 0)),
                  pl.BlockSpec((d, tn), lambda g, i, j: (0, g))],
        out_specs=pl.BlockSpec((None, tm, tn), lambda g, i, j: (i, j, g)),
        out_shape=jax.ShapeDtypeStruct((b, sq, n), BF16),
        compiler_params=_cparams(("arbitrary", "arbitrary", "arbitrary"), 56),
        name="attn_proj",
    )(x, mod, w_bf16)


def _flat_rows(v):
    return v.reshape(-1, v.shape[-1])


def _attn_block(q, kc, vc, kp, vp, bias):
    has_prev = kp is not None
    dh = ATTN_HEAD_DIM
    cols = [slice(h * dh, (h + 1) * dh) for h in range(ATTN_HEADS)]
    qb = q.shape[0]
    pb = kp.shape[0] if has_prev else 0
    lane = lax.broadcasted_iota(jnp.int32, (qb, LANES), 1)
    nt = (((1,), (1,)), ((), ()))

    outs, lses = [], []
    stage = ATT_HEADS_PER_STAGE_NO_PREV if kp is None else ATT_HEADS_PER_STAGE_PREV
    for h0 in range(0, ATTN_HEADS, stage):
        heads = range(h0, h0 + stage)
        scores = {}
        for h in heads:
            s = lax.dot_general(q[:, cols[h]], kc[:, cols[h]], nt, preferred_element_type=F32)
            if has_prev:
                sp = lax.dot_general(q[:, cols[h]], kp[:, cols[h]], nt, preferred_element_type=F32)
                s = jnp.concatenate([sp, s], axis=1)
            scores[h] = s
        probs, maxes = {}, {}
        for h in heads:
            t = scores[h] * (dh ** -0.5 * LOG2E) + bias
            maxes[h] = jnp.max(t, axis=1, keepdims=True)
            probs[h] = jnp.exp2(t - maxes[h]).astype(BF16)
        accs = {}
        for h in heads:
            acc = jnp.dot(probs[h][:, pb:], jnp.concatenate([vc[:, cols[h]], jnp.ones((qb, LANES), BF16)], axis=1),
                          preferred_element_type=F32)
            if has_prev:
                acc = acc + jnp.dot(probs[h][:, :pb],
                                    jnp.concatenate([vp[:, cols[h]], jnp.ones((pb, LANES), BF16)], axis=1),
                                    preferred_element_type=F32)
            accs[h] = acc
        outs += [(accs[h][:, :dh] / accs[h][:, dh:]).astype(BF16) for h in heads]
        lses += [maxes[h] * LN2 + jnp.log(accs[h][:, dh:]) for h in heads]
    lse_tile = lses[0]
    for h in range(1, ATTN_HEADS):
        lse_tile = jnp.where(lane == h, lses[h], lse_tile)
    return jnp.concatenate(outs, axis=1), lse_tile


def _attn_whole_body(q_ref, k_ref, v_ref, bias_ref, o_ref, lse_ref):
    for c in range(q_ref.shape[0]):
        o, lse = _attn_block(q_ref[c], k_ref[c], v_ref[c], None, None, bias_ref[1])
        o_ref[c] = o
        lse_ref[c] = lse


def _attn_chain_body(q_ref, k_ref, v_ref, kp_ref, vp_ref, bias_ref, o_ref, lse_ref, *, n_axis, nblk):
    n = pl.program_id(n_axis)
    nslab = q_ref.shape[0]
    ur = q_ref.shape[1] // nblk
    urp = kp_ref.shape[1]

    def part(ref, lo, hi):
        return _flat_rows(ref[:, lo:hi, :])

    for i in range(nblk):
        lo, hi = i * ur, (i + 1) * ur
        if i == 0:
            kp, vp, bias = _flat_rows(kp_ref[...]), _flat_rows(vp_ref[...]), bias_ref[jnp.minimum(n, 1)]
        else:
            kp, vp, bias = part(k_ref, lo - urp, lo), part(v_ref, lo - urp, lo), bias_ref[1]
        o, lse = _attn_block(part(q_ref, lo, hi), part(k_ref, lo, hi), part(v_ref, lo, hi), kp, vp, bias)
        o_ref[:, lo:hi, :] = o.reshape(nslab, ur, o.shape[-1])
        lse_ref[:, lo:hi, :] = lse.reshape(nslab, ur, LANES)


def _window_bias(nslab, cur_rows, prev_rows):
    qb = nslab * cur_rows
    pos_c = np.array([nslab * (i % cur_rows) + i // cur_rows for i in range(qb)])
    pb = nslab * prev_rows
    pos_p = np.array([nslab * (i % prev_rows) + i // prev_rows for i in range(pb)], dtype=np.int64) - pb
    kpos = np.concatenate([pos_p, pos_c])
    delta = pos_c[:, None] - kpos[None, :]
    ok = (delta >= 0) & (delta <= ATT_SPAN)
    later = np.where(ok, 0.0, -np.inf).astype(np.float32)
    first = later.copy()
    first[:, :pb] = -np.inf
    return jnp.asarray(np.stack([first, later]))


def _attn(proj, g, dil, d):
    b, ncls, u, _ = proj.shape
    qb = ATT_QB
    cq, ck, cv = 3 * g, 3 * g + 1, 3 * g + 2
    nsub = ATT_BLOCKS_PER_STEP
    if dil == NCLS:
        nsub = ATT_SEQS_PER_STEP
        assert u == qb
        bias = _window_bias(1, qb, 0)
        arrs = (proj,) * 3
        def blk(col, width=d):
            return pl.BlockSpec((None, nsub, qb, width), lambda i, r: (i, r, 0, col))
        in_specs = [blk(cq), blk(ck), blk(cv)]
        o_spec, l_spec = blk(0), blk(0, LANES)
        grid = (b, ncls // nsub)
        o_shape, l_shape = (b, ncls, u, d), (b, ncls, u, LANES)
        sem = ("parallel", "parallel")
        body = _attn_whole_body
    else:
        nslab = ncls // dil
        ur = qb // nslab
        urp = max(ATT_SPAN // nslab, 16)
        bias = _window_bias(nslab, ur, urp)
        p5 = proj.reshape(b, nslab, dil, u, proj.shape[-1])
        arrs = (p5,) * 5
        step = nsub * ur
        def blk(col, width=d):
            return pl.BlockSpec((None, nslab, None, step, width), lambda i, r, n: (i, 0, r, n, col))
        def prev(col):
            return pl.BlockSpec((None, nslab, None, urp, d),
                                lambda i, r, n: (i, 0, r, jnp.maximum((step // urp) * n - 1, 0), col))
        in_specs = [blk(cq), blk(ck), blk(cv), prev(ck), prev(cv)]
        o_spec, l_spec = blk(0), blk(0, LANES)
        grid = (b, dil, u // step)
        o_shape, l_shape = (b, nslab, dil, u, d), (b, nslab, dil, u, LANES)
        sem = ("parallel", "parallel", "arbitrary")
        body = functools.partial(_attn_chain_body, n_axis=2, nblk=nsub)
    bias_spec = pl.BlockSpec(bias.shape, lambda *_: (0, 0, 0))
    o, lse = pl.pallas_call(
        body,
        grid=grid,
        in_specs=in_specs + [bias_spec],
        out_specs=[o_spec, l_spec],
        out_shape=[jax.ShapeDtypeStruct(o_shape, BF16), jax.ShapeDtypeStruct(l_shape, F32)],
        compiler_params=_cparams(sem, 40),
        name=f"attn_d{dil}",
    )(*arrs, bias)
    return o.reshape(b, ncls * u, d), lse.reshape(b, ncls * u, LANES)


def _amerge_body(o1_ref, o4_ref, o16_ref, l1_ref, l4_ref, l16_ref, x_ref, mod_ref, lng_ref, lnb_ref,
                 w_ref, out_ref):
    dh = ATTN_HEAD_DIM
    heads = range(ATTN_HEADS)
    l1, l4, l16 = l1_ref[...], l4_ref[...], l16_ref[...]
    mx = jnp.maximum(jnp.maximum(l1, l4), l16)
    e1, e4, e16 = jnp.exp(l1 - mx), jnp.exp(l4 - mx), jnp.exp(l16 - mx)
    tot = e1 + e4 + e16
    w1 = e1 / tot
    w4 = e4 / tot
    w1_h = [w1[:, h:h + 1] for h in heads]
    w4_h = [w4[:, h:h + 1] for h in heads]
    ys = []
    for h in heads:
        cs = slice(h * dh, (h + 1) * dh)
        o16 = o16_ref[:, cs].astype(F32)
        acc = o16 + w1_h[h] * (o1_ref[:, cs].astype(F32) - o16) + w4_h[h] * (o4_ref[:, cs].astype(F32) - o16)
        ys.append(acc.astype(BF16))
    out = jnp.dot(jnp.concatenate(ys, axis=1), w_ref[...], preferred_element_type=F32)
    out_ref[...] = _post(x_ref[...], out, mod_ref, 1, 1.0, lng_ref, lnb_ref)


def _attn_merge_out(o_list, lse_list, x, mod, lng, lnb, w_out, tm=1024):
    b, sq, d = x.shape
    w = w_out.astype(BF16)
    row = pl.BlockSpec((None, tm, d), lambda i, j: (i, j, 0))
    lrow = pl.BlockSpec((None, tm, LANES), lambda i, j: (i, j, 0))
    return pl.pallas_call(
        _amerge_body,
        grid=(b, sq // tm),
        in_specs=[row, row, row, lrow, lrow, lrow, row,
                  pl.BlockSpec((None, 3 * N_SUB, d), lambda i, j: (i, 0, 0)),
                  _resident((1, d)), _resident((1, d)), _resident(w.shape)],
        out_specs=row,
        out_shape=jax.ShapeDtypeStruct(x.shape, F32),
        compiler_params=_cparams(("parallel", "parallel"), 56),
        name="attn_merge_out",
    )(*o_list, *lse_list, x, mod, lng.reshape(1, d), lnb.reshape(1, d), w)


def kernel(x, c, ada_w, ada_b, ln_g, ln_b, ffn_w_in, ffn_w_out, mlstm_w_in, mlstm_gate_bias, mlstm_conv_w,
           mlstm_head_gain, mlstm_w_out, attn_w_in, attn_w_out):
    b, sq, d = x.shape
    assert all(w // dl == ATT_SPAN for w, dl in DIL_GROUPS) and [dl for _, dl in DIL_GROUPS] == [1, 4, NCLS]
    assert sq == NCLS * ATT_QB and d == MLSTM_HEADS * MLSTM_DV == ATTN_HEADS * ATTN_HEAD_DIM
    mod_all = _ada(c, ada_w, ada_b)
    for layer in range(DEPTH):
        mod = mod_all[layer].reshape(b, 3 * N_SUB, d)
        j = layer // 2
        if layer % 2 == 0:
            x = _ffn(x, mod, ln_g[layer, 0], ln_b[layer, 0], ffn_w_in[layer, 0], ffn_w_out[layer, 0], 0)
            q, kt, v, o, bc, cm, gt = _mlstm_proj(x, mod, mlstm_w_in[j], mlstm_gate_bias[j], mlstm_conv_w[j])
            y = _mlstm_cell(q, kt, v, o, bc, cm, gt, mlstm_head_gain[j])
            x = _mixer_out(y, x, mod, ln_g[layer, 1], ln_b[layer, 1], mlstm_w_out[j])
            x = _ffn(x, mod, ln_g[layer, 2], ln_b[layer, 2], ffn_w_in[layer, 1], ffn_w_out[layer, 1], 2)
        else:
            xc = _ffn(x, mod, ln_g[layer, 0], ln_b[layer, 0], ffn_w_in[layer, 0], ffn_w_out[layer, 0], 0,
                      order_out="cls")
            xf = xc.reshape(b, sq, d)
            proj = _attn_proj(xf, mod, attn_w_in[j].astype(BF16)).reshape(b, NCLS, sq // NCLS, -1)
            outs, lses = [], []
            for g, (_, dil) in enumerate(DIL_GROUPS):
                o_g, lse_g = _attn(proj, g, dil, d)
                outs.append(o_g)
                lses.append(lse_g)
            xf = _attn_merge_out(outs, lses, xf, mod, ln_g[layer, 1], ln_b[layer, 1], attn_w_out[j])
            x = _ffn(xf.reshape(b, NCLS, sq // NCLS, d), mod, ln_g[layer, 2], ln_b[layer, 2],
                     ffn_w_in[layer, 1], ffn_w_out[layer, 1], 2, order_in="cls")
    return x
```

```python
import functools

import numpy as np
import jax
import jax.numpy as jnp
from jax import lax
from jax.experimental import pallas as pl
from jax.experimental.pallas import tpu as pltpu

F32 = jnp.float32
BF16 = jnp.bfloat16

DEPTH = 2
N_SUB = 3
D_FF = 2816
MLSTM_HEADS = 8
MLSTM_DQK = 64
MLSTM_DV = 128
CONV_WIDTH = 4
DIL_GROUPS = ((128, 1), (512, 4), (2048, 16))
ATTN_HEADS = 8
ATTN_HEAD_DIM = 128
ALPHA = (2 * DEPTH) ** 0.25
LN_EPS = 1e-5
RMS_EPS = 1e-6
LOG2E = 1.4426950408889634
LN2 = 0.6931471805599453

LANES = 128
NCLS = 16
CELL_CHUNK = 128
CELL_CHUNKS_PER_STEP = 4
OUT_SUBTILE_ROWS = 512
FFN_SUBTILE_ROWS = 512
CELL_HEADS_PER_STAGE = 8
ATT_SPAN = 128
ATT_QB = 256
ATT_BLOCKS_PER_STEP = 4
ATT_SEQS_PER_STEP = 8
ATT_HEADS_PER_STAGE_NO_PREV = 4
ATT_HEADS_PER_STAGE_PREV = 8
MIB = 1024 * 1024


def _cparams(sem, vmem_mib):
    return pltpu.CompilerParams(dimension_semantics=sem, vmem_limit_bytes=int(vmem_mib * MIB))


def _resident(shape):
    nd = len(shape)
    return pl.BlockSpec(shape, lambda *_: (0,) * nd, pipeline_mode=pl.Buffered(1))


def _layer_norm(y, g, b):
    mu = jnp.mean(y, axis=-1, keepdims=True)
    yc = y - mu
    var = jnp.mean(yc * yc, axis=-1, keepdims=True)
    return yc * lax.rsqrt(var + LN_EPS) * g + b


def _modulate(x, mod_ref, s):
    return x * (1.0 + mod_ref[3 * s + 1:3 * s + 2, :]) + mod_ref[3 * s:3 * s + 1, :]


def _post(x, out, mod_ref, s, weight, lng_ref, lnb_ref):
    y = ALPHA * x + (weight * (1.0 + mod_ref[3 * s + 2:3 * s + 3, :])) * out
    return _layer_norm(y, lng_ref[...], lnb_ref[...])


def _ada_body(c_ref, w_ref, b_ref, o_ref):
    c = c_ref[...]
    cond = c * jax.nn.sigmoid(c)
    o_ref[...] = jnp.dot(cond, w_ref[...], precision=lax.Precision.HIGHEST,
                         preferred_element_type=F32) + b_ref[...]


def _ada(c, ada_w, ada_b):
    depth, d, n = ada_w.shape
    b = c.shape[0]
    tn = 1152
    return pl.pallas_call(
        _ada_body,
        grid=(depth, n // tn),
        in_specs=[pl.BlockSpec((b, d), lambda l, j: (0, 0)),
                  pl.BlockSpec((None, d, tn), lambda l, j: (l, 0, j)),
                  pl.BlockSpec((None, 1, tn), lambda l, j: (l, 0, j))],
        out_specs=pl.BlockSpec((None, b, tn), lambda l, j: (l, 0, j)),
        out_shape=jax.ShapeDtypeStruct((depth, b, n), F32),
        compiler_params=_cparams(("arbitrary", "arbitrary"), 32),
        name="ada_mod",
    )(c, ada_w, ada_b.reshape(depth, 1, n))


def _ffn_body(x_ref, mod_ref, lng_ref, lnb_ref, wg_ref, wu_ref, wo_ref, o_ref, *, s, order_in, order_out):
    st = FFN_SUBTILE_ROWS
    per = st // NCLS
    d = x_ref.shape[-1]
    tm = x_ref.shape[0] * x_ref.shape[1] if order_in == "cls" else x_ref.shape[0]
    for k in range(tm // st):
        nat_rows = slice(k * st, (k + 1) * st)
        cls_rows = slice(k * per, (k + 1) * per)
        if order_in == "cls":
            x = x_ref[:, cls_rows, :].reshape(st, d)
        elif order_out == "cls":
            x = jnp.swapaxes(x_ref[nat_rows, :].reshape(per, NCLS, d), 0, 1).reshape(st, d)
        else:
            x = x_ref[nat_rows, :]
        h = _modulate(x, mod_ref, s).astype(BF16)
        g = jnp.dot(h, wg_ref[...], preferred_element_type=F32)
        u = jnp.dot(h, wu_ref[...], preferred_element_type=F32)
        a = (g * jax.nn.sigmoid(g) * u).astype(BF16)
        out = jnp.dot(a, wo_ref[...], preferred_element_type=F32)
        y = _post(x, out, mod_ref, s, 0.5, lng_ref, lnb_ref)
        if order_out == "cls":
            o_ref[:, cls_rows, :] = y.reshape(NCLS, per, d)
        elif order_in == "cls":
            o_ref[nat_rows, :] = jnp.swapaxes(y.reshape(NCLS, per, d), 0, 1).reshape(st, d)
        else:
            o_ref[nat_rows, :] = y


def _ffn(x, mod, lng, lnb, w_in, w_out, s, order_in="nat", order_out="nat", tm=1024):
    d = x.shape[-1]
    b = x.shape[0]
    sq = x.shape[1] if order_in == "nat" else x.shape[1] * x.shape[2]
    wg = w_in[:, :D_FF].astype(BF16)
    wu = w_in[:, D_FF:].astype(BF16)
    wo = w_out.astype(BF16)
    nat = pl.BlockSpec((None, tm, d), lambda i, j: (i, j, 0))
    cls = pl.BlockSpec((None, NCLS, tm // NCLS, d), lambda i, j: (i, 0, j, 0))
    out_shape = (b, sq, d) if order_out == "nat" else (b, NCLS, sq // NCLS, d)
    return pl.pallas_call(
        functools.partial(_ffn_body, s=s, order_in=order_in, order_out=order_out),
        grid=(b, sq // tm),
        in_specs=[cls if order_in == "cls" else nat,
                  pl.BlockSpec((None, 3 * N_SUB, d), lambda i, j: (i, 0, 0)),
                  _resident((1, d)), _resident((1, d)),
                  _resident(wg.shape), _resident(wu.shape), _resident(wo.shape)],
        out_specs=cls if order_out == "cls" else nat,
        out_shape=jax.ShapeDtypeStruct(out_shape, F32),
        compiler_params=_cparams(("parallel", "parallel"), 56),
        name=f"ffn_s{s}_{order_in}_{order_out}",
    )(x, mod, lng.reshape(1, d), lnb.reshape(1, d), wg, wu, wo)


def _mproj_body(x_ref, mod_ref, w_ref, wgate_ref, gbias_ref, convw_ref,
                q_ref, kt_ref, v_ref, o_ref, bc_ref, cm_ref, gt_ref, pbuf):
    tm, d = x_ref.shape
    nh, dk = MLSTM_HEADS, MLSTM_DQK
    L = CELL_CHUNK
    j = pl.program_id(1)

    @pl.when(j == 0)
    def _():
        pbuf[0:8, :] = jnp.zeros((8, d), F32)

    @pl.when(j > 0)
    def _():
        pbuf[0:8, :] = pbuf[tm:tm + 8, :]

    h = _modulate(x_ref[...], mod_ref, 1).astype(BF16)
    p_qk = jnp.dot(h, w_ref[:, :d], preferred_element_type=F32)
    gg = jnp.dot(h, wgate_ref[...], preferred_element_type=F32) + gbias_ref[...]
    v_ref[...] = jnp.dot(h, w_ref[:, d:2 * d], preferred_element_type=F32).astype(BF16)

    pbuf[8:tm + 8, :] = p_qk
    last = CONV_WIDTH - 1
    conv = convw_ref[last:last + 1, :] * pbuf[8:8 + tm, :]
    for i in range(last - 1, -1, -1):
        conv = conv + convw_ref[i:i + 1, :] * pbuf[8 - last + i:8 - last + i + tm, :]
    qk = conv * jax.nn.sigmoid(conv)
    lane = lax.broadcasted_iota(jnp.int32, (tm, LANES), 1)
    for hd in range(nh):
        pair = qk[:, (hd // 2) * LANES:(hd // 2 + 1) * LANES] * (dk ** -0.5)
        keep = (lane >= (hd % 2) * dk) & (lane < (hd % 2 + 1) * dk)
        q_ref[:, hd * LANES:(hd + 1) * LANES] = jnp.where(keep, pair, 0.0).astype(BF16)
    kt_ref[...] = qk[:, nh * dk:].T.astype(BF16)

    o_ref[...] = jnp.dot(h, w_ref[:, 2 * d:], preferred_element_type=F32).astype(BF16)

    ig = gg[:, :LANES]
    fp = gg[:, LANES:]
    logf = jnp.minimum(fp, 0.0) - jnp.log1p(jnp.exp(-jnp.abs(fp)))
    ri = lax.broadcasted_iota(jnp.int32, (L, L), 0)
    ci = lax.broadcasted_iota(jnp.int32, (L, L), 1)
    tril = jnp.where(ri >= ci, 1.0, 0.0).astype(F32)
    b = jnp.concatenate(
        [jnp.dot(tril, logf[c * L:(c + 1) * L, :], precision=lax.Precision.HIGHEST, preferred_element_type=F32)
         for c in range(tm // L)], axis=0)
    g = ig - b
    row_in_chunk = lax.broadcasted_iota(jnp.int32, (tm, LANES), 0) & (L - 1)
    cm = g
    shift = 1
    while shift < L:
        cm = jnp.maximum(cm, jnp.where(row_in_chunk >= shift, pltpu.roll(cm, shift, 0), -jnp.inf))
        shift *= 2
    bc_ref[...] = b
    cm_ref[...] = cm
    gt_ref[...] = g.T[:nh, :]


def _mlstm_proj(x, mod, w_in, gate_bias, conv_w, tm=1024):
    b, sq, d = x.shape
    n_main = 3 * d
    n_gate = 2 * MLSTM_HEADS
    nk = MLSTM_HEADS * MLSTM_DQK
    w = w_in[:, :n_main].astype(BF16)
    nh = MLSTM_HEADS
    pad = ((0, 0), (0, LANES - nh))
    wgate = jnp.concatenate([jnp.pad(w_in[:, n_main:n_main + nh], pad),
                             jnp.pad(w_in[:, n_main + nh:], pad)], axis=1).astype(BF16)
    gb = gate_bias.astype(F32).reshape(1, n_gate)
    gbias = jnp.concatenate([jnp.pad(gb[:, :nh], pad), jnp.pad(gb[:, nh:], pad)], axis=1)
    row = pl.BlockSpec((None, tm, d), lambda i, j: (i, j, 0))
    lrow = pl.BlockSpec((None, tm, LANES), lambda i, j: (i, j, 0))
    act = jax.ShapeDtypeStruct((b, sq, d), BF16)
    gate_tile = jax.ShapeDtypeStruct((b, sq, LANES), F32)
    return pl.pallas_call(
        _mproj_body,
        grid=(b, sq // tm),
        in_specs=[row,
                  pl.BlockSpec((None, 3 * N_SUB, d), lambda i, j: (i, 0, 0)),
                  _resident(w.shape), _resident(wgate.shape), _resident(gbias.shape),
                  _resident((CONV_WIDTH, d))],
        out_specs=[row,
                   pl.BlockSpec((None, nk, tm), lambda i, j: (i, 0, j)),
                   row, row, lrow, lrow,
                   pl.BlockSpec((None, nh, tm), lambda i, j: (i, 0, j))],
        out_shape=[act, jax.ShapeDtypeStruct((b, nk, sq), BF16), act, act, gate_tile, gate_tile,
                   jax.ShapeDtypeStruct((b, nh, sq), F32)],
        scratch_shapes=[pltpu.VMEM((tm + 8, d), F32)],
        compiler_params=_cparams(("parallel", "arbitrary"), 56),
        name="mlstm_proj",
    )(x, mod, w, wgate, gbias, conv_w.astype(F32))


def _cell_body(q_ref, kt_ref, v_ref, o_ref, bc_ref, cm_ref, gt_ref, gain_ref, y_ref, cst, mst):
    @pl.when(pl.program_id(1) == 0)
    def _():
        cst[...] = jnp.zeros(cst.shape, F32)
        mst[...] = jnp.zeros(mst.shape, F32)

    for s in range(q_ref.shape[0] // CELL_CHUNK):
        _cell_chunk(s, q_ref, kt_ref, v_ref, o_ref, bc_ref, cm_ref, gt_ref, gain_ref, y_ref, cst, mst)


def _cell_chunk(s, q_ref, kt_ref, v_ref, o_ref, bc_ref, cm_ref, gt_ref, gain_ref, y_ref, cst, mst):
    L = CELL_CHUNK
    rows = slice(s * L, (s + 1) * L)
    nh, dk, dv = MLSTM_HEADS, MLSTM_DQK, MLSTM_DV
    heads = range(nh)

    m_prev = mst[...]
    bt = bc_ref[rows, :]
    cmt = cm_ref[rows, :]
    mt = jnp.maximum(m_prev, cmt)
    iwt = jnp.exp(m_prev - mt)
    emt = jnp.exp(-(bt + mt))
    b_last = bt[L - 1:L, :]
    cm_last = cmt[L - 1:L, :]
    m_loc = b_last + cm_last
    m_new = jnp.maximum(b_last + m_prev, m_loc)
    sp = jnp.exp(b_last + m_prev - m_new)
    sl = jnp.exp(m_loc - m_new)

    ri = lax.broadcasted_iota(jnp.int32, (L, L), 0)
    ci = lax.broadcasted_iota(jnp.int32, (L, L), 1)
    causal = ri >= ci
    ones = jnp.ones((L, LANES), BF16)
    kt_p = [kt_ref[p * LANES:(p + 1) * LANES, rows] for p in range(nh // 2)]
    outs = {}
    for h0 in range(0, nh, CELL_HEADS_PER_STAGE):
        heads = range(h0, h0 + CELL_HEADS_PER_STAGE)
        g_row = {h: gt_ref[h:h + 1, rows] for h in heads}
        m_col = {h: mt[:, h:h + 1] for h in heads}
        dmat = {h: jnp.exp(jnp.where(causal, g_row[h] - m_col[h], -jnp.inf)) for h in heads}
        wa = {h: jnp.exp(g_row[h] - cm_last[:, h:h + 1]) for h in heads}

        q_h = {h: q_ref[rows, h * LANES:(h + 1) * LANES] for h in heads}
        v_aug = {h: jnp.concatenate([v_ref[rows, h * dv:(h + 1) * dv], ones], axis=1) for h in heads}
        c_pair = {p: jnp.concatenate([cst[2 * p], cst[2 * p + 1]], axis=0).astype(BF16)
                  for p in range(h0 // 2, (h0 + CELL_HEADS_PER_STAGE) // 2)}

        scores = {h: jnp.dot(q_h[h], kt_p[h // 2], preferred_element_type=F32) for h in heads}
        inter = {h: jnp.dot(q_h[h], c_pair[h // 2], preferred_element_type=F32) for h in heads}
        pm = {h: (scores[h] * dmat[h]).astype(BF16) for h in heads}
        iw = {h: iwt[:, h:h + 1] for h in heads}
        num_aug = {h: jnp.dot(pm[h], v_aug[h], preferred_element_type=F32) + iw[h] * inter[h] for h in heads}

        em = {h: emt[:, h:h + 1] for h in heads}
        hh = {h: num_aug[h][:, :dv] / jnp.maximum(jnp.abs(num_aug[h][:, dv:]), em[h]) for h in heads}
        ms = {h: jnp.mean(hh[h] * hh[h], axis=-1, keepdims=True) for h in heads}
        rs = {h: lax.rsqrt(ms[h] + RMS_EPS) for h in heads}
        hn = {h: hh[h] * rs[h] * gain_ref[:, h * dv:(h + 1) * dv] for h in heads}
        og = {h: o_ref[rows, h * dv:(h + 1) * dv].astype(F32) for h in heads}
        outs.update({h: (jax.nn.sigmoid(og[h]) * hn[h]).astype(BF16) for h in heads})

        kw = {h: (kt_p[h // 2][(h % 2) * dk:(h % 2 + 1) * dk, :].astype(F32) * wa[h]).astype(BF16) for h in heads}
        c_loc = {h: jnp.dot(kw[h], v_aug[h], preferred_element_type=F32) for h in heads}
        for h in heads:
            cst[h] = sp[:, h:h + 1] * cst[h] + sl[:, h:h + 1] * c_loc[h]
    y_ref[rows, :] = jnp.concatenate([outs[h] for h in range(nh)], axis=1)
    mst[...] = m_new


def _mlstm_cell(q, kt, v, o, bc, cm, gt, head_gain):
    b, sq, d = v.shape
    L = CELL_CHUNKS_PER_STEP * CELL_CHUNK
    nk = kt.shape[1]
    row = pl.BlockSpec((None, L, d), lambda i, j: (i, j, 0))
    lrow = pl.BlockSpec((None, L, LANES), lambda i, j: (i, j, 0))
    return pl.pallas_call(
        _cell_body,
        grid=(b, sq // L),
        in_specs=[row,
                  pl.BlockSpec((None, nk, L), lambda i, j: (i, 0, j)),
                  row, row, lrow, lrow,
                  pl.BlockSpec((None, MLSTM_HEADS, L), lambda i, j: (i, 0, j)),
                  _resident((1, d))],
        out_specs=row,
        out_shape=jax.ShapeDtypeStruct((b, sq, d), BF16),
        scratch_shapes=[pltpu.VMEM((MLSTM_HEADS, MLSTM_DQK, 2 * MLSTM_DV), F32),
                        pltpu.VMEM((1, LANES), F32)],
        compiler_params=_cparams(("parallel", "arbitrary"), 32),
        name="mlstm_cell",
    )(q, kt, v, o, bc, cm, gt, head_gain.astype(F32).reshape(1, d))


def _mixer_out_body(y_ref, x_ref, mod_ref, lng_ref, lnb_ref, w_ref, o_ref):
    st = OUT_SUBTILE_ROWS
    for k in range(y_ref.shape[0] // st):
        rows = slice(k * st, (k + 1) * st)
        out = jnp.dot(y_ref[rows, :], w_ref[...], preferred_element_type=F32)
        o_ref[rows, :] = _post(x_ref[rows, :], out, mod_ref, 1, 1.0, lng_ref, lnb_ref)


def _mixer_out(y, x, mod, lng, lnb, w_out, tm=1024):
    b, sq, d = x.shape
    w = w_out.astype(BF16)
    row = pl.BlockSpec((None, tm, d), lambda i, j: (i, j, 0))
    return pl.pallas_call(
        _mixer_out_body,
        grid=(b, sq // tm),
        in_specs=[row, row,
                  pl.BlockSpec((None, 3 * N_SUB, d), lambda i, j: (i, 0, 0)),
                  _resident((1, d)), _resident((1, d)), _resident(w.shape)],
        out_specs=row,
        out_shape=jax.ShapeDtypeStruct(x.shape, F32),
        compiler_params=_cparams(("parallel", "parallel"), 48),
        name="mlstm_out",
    )(y, x, mod, lng.reshape(1, d), lnb.reshape(1, d), w)


def _aproj_body(x_ref, mod_ref, w_ref, o_ref):
    h = _modulate(x_ref[...], mod_ref, 1).astype(BF16)
    o_ref[...] = jnp.dot(h, w_ref[...], preferred_element_type=F32).astype(BF16)


def _attn_proj(x, mod, w_bf16, tm=1024, tn=3072):
    b, sq, d = x.shape
    n = w_bf16.shape[1]
    return pl.pallas_call(
        _aproj_body,
        grid=(n // tn, b, sq // tm),
        in_specs=[pl.BlockSpec((None, tm, d), lambda g, i, j: (i, j, 0)),
                  pl.BlockSpec((None, 3 * N_SUB, d), lambda g, i, j: (i, 0, 0)),
                  pl.BlockSpec((d, tn), lambda g, i, j: (0, g))],
        out_specs=pl.BlockSpec((None, tm, tn), lambda g, i, j: (i, j, g)),
        out_shape=jax.ShapeDtypeStruct((b, sq, n), BF16),
        compiler_params=_cparams(("arbitrary", "arbitrary", "arbitrary"), 56),
        name="attn_proj",
    )(x, mod, w_bf16)


def _flat_rows(v):
    return v.reshape(-1, v.shape[-1])


def _attn_block(q, kc, vc, kp, vp, bias):
    has_prev = kp is not None
    dh = ATTN_HEAD_DIM
    cols = [slice(h * dh, (h + 1) * dh) for h in range(ATTN_HEADS)]
    qb = q.shape[0]
    pb = kp.shape[0] if has_prev else 0
    lane = lax.broadcasted_iota(jnp.int32, (qb, LANES), 1)
    nt = (((1,), (1,)), ((), ()))

    outs, lses = [], []
    stage = ATT_HEADS_PER_STAGE_NO_PREV if kp is None else ATT_HEADS_PER_STAGE_PREV
    for h0 in range(0, ATTN_HEADS, stage):
        heads = range(h0, h0 + stage)
        scores = {}
        for h in heads:
            s = lax.dot_general(q[:, cols[h]], kc[:, cols[h]], nt, preferred_element_type=F32)
            if has_prev:
                sp = lax.dot_general(q[:, cols[h]], kp[:, cols[h]], nt, preferred_element_type=F32)
                s = jnp.concatenate([sp, s], axis=1)
            scores[h] = s
        probs, maxes = {}, {}
        for h in heads:
            t = scores[h] * (dh ** -0.5 * LOG2E) + bias
            maxes[h] = jnp.max(t, axis=1, keepdims=True)
            probs[h] = jnp.exp2(t - maxes[h]).astype(BF16)
        accs = {}
        for h in heads:
            acc = jnp.dot(probs[h][:, pb:], jnp.concatenate([vc[:, cols[h]], jnp.ones((qb, LANES), BF16)], axis=1),
                          preferred_element_type=F32)
            if has_prev:
                acc = acc + jnp.dot(probs[h][:, :pb],
                                    jnp.concatenate([vp[:, cols[h]], jnp.ones((pb, LANES), BF16)], axis=1),
                                    preferred_element_type=F32)
            accs[h] = acc
        outs += [(accs[h][:, :dh] / accs[h][:, dh:]).astype(BF16) for h in heads]
        lses += [maxes[h] * LN2 + jnp.log(accs[h][:, dh:]) for h in heads]
    lse_tile = lses[0]
    for h in range(1, ATTN_HEADS):
        lse_tile = jnp.where(lane == h, lses[h], lse_tile)
    return jnp.concatenate(outs, axis=1), lse_tile


def _attn_whole_body(q_ref, k_ref, v_ref, bias_ref, o_ref, lse_ref):
    for c in range(q_ref.shape[0]):
        o, lse = _attn_block(q_ref[c], k_ref[c], v_ref[c], None, None, bias_ref[1])
        o_ref[c] = o
        lse_ref[c] = lse


def _attn_chain_body(q_ref, k_ref, v_ref, kp_ref, vp_ref, bias_ref, o_ref, lse_ref, *, n_axis, nblk):
    n = pl.program_id(n_axis)
    nslab = q_ref.shape[0]
    ur = q_ref.shape[1] // nblk
    urp = kp_ref.shape[1]

    def part(ref, lo, hi):
        return _flat_rows(ref[:, lo:hi, :])

    for i in range(nblk):
        lo, hi = i * ur, (i + 1) * ur
        if i == 0:
            kp, vp, bias = _flat_rows(kp_ref[...]), _flat_rows(vp_ref[...]), bias_ref[jnp.minimum(n, 1)]
        else:
            kp, vp, bias = part(k_ref, lo - urp, lo), part(v_ref, lo - urp, lo), bias_ref[1]
        o, lse = _attn_block(part(q_ref, lo, hi), part(k_ref, lo, hi), part(v_ref, lo, hi), kp, vp, bias)
        o_ref[:, lo:hi, :] = o.reshape(nslab, ur, o.shape[-1])
        lse_ref[:, lo:hi, :] = lse.reshape(nslab, ur, LANES)


def _window_bias(nslab, cur_rows, prev_rows):
    qb = nslab * cur_rows
    pos_c = np.array([nslab * (i % cur_rows) + i // cur_rows for i in range(qb)])
    pb = nslab * prev_rows
    pos_p = np.array([nslab * (i % prev_rows) + i // prev_rows for i in range(pb)], dtype=np.int64) - pb
    kpos = np.concatenate([pos_p, pos_c])
    delta = pos_c[:, None] - kpos[None, :]
    ok = (delta >= 0) & (delta <= ATT_SPAN)
    later = np.where(ok, 0.0, -np.inf).astype(np.float32)
    first = later.copy()
    first[:, :pb] = -np.inf
    return jnp.asarray(np.stack([first, later]))


def _attn(proj, g, dil, d):
    b, ncls, u, _ = proj.shape
    qb = ATT_QB
    cq, ck, cv = 3 * g, 3 * g + 1, 3 * g + 2
    nsub = ATT_BLOCKS_PER_STEP
    if dil == NCLS:
        nsub = ATT_SEQS_PER_STEP
        assert u == qb
        bias = _window_bias(1, qb, 0)
        arrs = (proj,) * 3
        def blk(col, width=d):
            return pl.BlockSpec((None, nsub, qb, width), lambda i, r: (i, r, 0, col))
        in_specs = [blk(cq), blk(ck), blk(cv)]
        o_spec, l_spec = blk(0), blk(0, LANES)
        grid = (b, ncls // nsub)
        o_shape, l_shape = (b, ncls, u, d), (b, ncls, u, LANES)
        sem = ("parallel", "parallel")
        body = _attn_whole_body
    else:
        nslab = ncls // dil
        ur = qb // nslab
        urp = max(ATT_SPAN // nslab, 16)
        bias = _window_bias(nslab, ur, urp)
        p5 = proj.reshape(b, nslab, dil, u, proj.shape[-1])
        arrs = (p5,) * 5
        step = nsub * ur
        def blk(col, width=d):
            return pl.BlockSpec((None, nslab, None, step, width), lambda i, r, n: (i, 0, r, n, col))
        def prev(col):
            return pl.BlockSpec((None, nslab, None, urp, d),
                                lambda i, r, n: (i, 0, r, jnp.maximum((step // urp) * n - 1, 0), col))
        in_specs = [blk(cq), blk(ck), blk(cv), prev(ck), prev(cv)]
        o_spec, l_spec = blk(0), blk(0, LANES)
        grid = (b, dil, u // step)
        o_shape, l_shape = (b, nslab, dil, u, d), (b, nslab, dil, u, LANES)
        sem = ("parallel", "parallel", "arbitrary")
        body = functools.partial(_attn_chain_body, n_axis=2, nblk=nsub)
    bias_spec = pl.BlockSpec(bias.shape, lambda *_: (0, 0, 0))
    o, lse = pl.pallas_call(
        body,
        grid=grid,
        in_specs=in_specs + [bias_spec],
        out_specs=[o_spec, l_spec],
        out_shape=[jax.ShapeDtypeStruct(o_shape, BF16), jax.ShapeDtypeStruct(l_shape, F32)],
        compiler_params=_cparams(sem, 40),
        name=f"attn_d{dil}",
    )(*arrs, bias)
    return o.reshape(b, ncls * u, d), lse.reshape(b, ncls * u, LANES)


def _amerge_body(o1_ref, o4_ref, o16_ref, l1_ref, l4_ref, l16_ref, x_ref, mod_ref, lng_ref, lnb_ref,
                 w_ref, out_ref):
    dh = ATTN_HEAD_DIM
    heads = range(ATTN_HEADS)
    l1, l4, l16 = l1_ref[...], l4_ref[...], l16_ref[...]
    mx = jnp.maximum(jnp.maximum(l1, l4), l16)
    e1, e4, e16 = jnp.exp(l1 - mx), jnp.exp(l4 - mx), jnp.exp(l16 - mx)
    tot = e1 + e4 + e16
    w1 = e1 / tot
    w4 = e4 / tot
    w1_h = [w1[:, h:h + 1] for h in heads]
    w4_h = [w4[:, h:h + 1] for h in heads]
    ys = []
    for h in heads:
        cs = slice(h * dh, (h + 1) * dh)
        o16 = o16_ref[:, cs].astype(F32)
        acc = o16 + w1_h[h] * (o1_ref[:, cs].astype(F32) - o16) + w4_h[h] * (o4_ref[:, cs].astype(F32) - o16)
        ys.append(acc.astype(BF16))
    out = jnp.dot(jnp.concatenate(ys, axis=1), w_ref[...], preferred_element_type=F32)
    out_ref[...] = _post(x_ref[...], out, mod_ref, 1, 1.0, lng_ref, lnb_ref)


def _attn_merge_out(o_list, lse_list, x, mod, lng, lnb, w_out, tm=1024):
    b, sq, d = x.shape
    w = w_out.astype(BF16)
    row = pl.BlockSpec((None, tm, d), lambda i, j: (i, j, 0))
    lrow = pl.BlockSpec((None, tm, LANES), lambda i, j: (i, j, 0))
    return pl.pallas_call(
        _amerge_body,
        grid=(b, sq // tm),
        in_specs=[row, row, row, lrow, lrow, lrow, row,
                  pl.BlockSpec((None, 3 * N_SUB, d), lambda i, j: (i, 0, 0)),
                  _resident((1, d)), _resident((1, d)), _resident(w.shape)],
        out_specs=row,
        out_shape=jax.ShapeDtypeStruct(x.shape, F32),
        compiler_params=_cparams(("parallel", "parallel"), 56),
        name="attn_merge_out",
    )(*o_list, *lse_list, x, mod, lng.reshape(1, d), lnb.reshape(1, d), w)


def kernel(x, c, ada_w, ada_b, ln_g, ln_b, ffn_w_in, ffn_w_out, mlstm_w_in, mlstm_gate_bias, mlstm_conv_w,
           mlstm_head_gain, mlstm_w_out, attn_w_in, attn_w_out):
    b, sq, d = x.shape
    assert all(w // dl == ATT_SPAN for w, dl in DIL_GROUPS) and [dl for _, dl in DIL_GROUPS] == [1, 4, NCLS]
    assert sq == NCLS * ATT_QB and d == MLSTM_HEADS * MLSTM_DV == ATTN_HEADS * ATTN_HEAD_DIM
    mod_all = _ada(c, ada_w, ada_b)
    for layer in range(DEPTH):
        mod = mod_all[layer].reshape(b, 3 * N_SUB, d)
        j = layer // 2
        if layer % 2 == 0:
            x = _ffn(x, mod, ln_g[layer, 0], ln_b[layer, 0], ffn_w_in[layer, 0], ffn_w_out[layer, 0], 0)
            q, kt, v, o, bc, cm, gt = _mlstm_proj(x, mod, mlstm_w_in[j], mlstm_gate_bias[j], mlstm_conv_w[j])
            y = _mlstm_cell(q, kt, v, o, bc, cm, gt, mlstm_head_gain[j])
            x = _mixer_out(y, x, mod, ln_g[layer, 1], ln_b[layer, 1], mlstm_w_out[j])
            x = _ffn(x, mod, ln_g[layer, 2], ln_b[layer, 2], ffn_w_in[layer, 1], ffn_w_out[layer, 1], 2)
        else:
            xc = _ffn(x, mod, ln_g[layer, 0], ln_b[layer, 0], ffn_w_in[layer, 0], ffn_w_out[layer, 0], 0,
                      order_out="cls")
            xf = xc.reshape(b, sq, d)
            proj = _attn_proj(xf, mod, attn_w_in[j].astype(BF16)).reshape(b, NCLS, sq // NCLS, -1)
            outs, lses = [], []
            for g, (_, dil) in enumerate(DIL_GROUPS):
                o_g, lse_g = _attn(proj, g, dil, d)
                outs.append(o_g)
                lses.append(lse_g)
            xf = _attn_merge_out(outs, lses, xf, mod, ln_g[layer, 1], ln_b[layer, 1], attn_w_out[j])
            x = _ffn(xf.reshape(b, NCLS, sq // NCLS, d), mod, ln_g[layer, 2], ln_b[layer, 2],
                     ffn_w_in[layer, 1], ffn_w_out[layer, 1], 2, order_in="cls")
    return x
```
